```python
import jax, jax.numpy as jnp
from jax import lax
import numpy as np

D_MODEL = 1024
BATCH = 2
SEQ = 8192
DEPTH = 1

MIX_WIDTH = D_MODEL
HEAD_DIM = 64
ATTN_WIDTH = MIX_WIDTH // 2
N_Q_HEADS = ATTN_WIDTH // HEAD_DIM
N_KV_HEADS = 2
KV_WIDTH = N_KV_HEADS * HEAD_DIM
TOK_WIDTH = MIX_WIDTH - ATTN_WIDTH
N_TOK_HEADS = TOK_WIDTH // HEAD_DIM
N_IN = ATTN_WIDTH + 2 * KV_WIDTH + 2 * TOK_WIDTH
CHUNK = 128
Q_BLOCK = 128
GRID_W = 64
ROPE_THETA = 10000.0
D_FF = 2816
CONV_W = 3
N_MOD = 6
EPS = 1e-6

kernel_name = "hybrid_attn_gmlp_convffn_encoder_layer"


def rmsnorm(x, g):
    xf = x.astype(jnp.float32)
    y = xf * lax.rsqrt(jnp.mean(xf * xf, axis=-1, keepdims=True) + EPS)
    return (y * g.astype(jnp.float32)).astype(x.dtype)


def axial_rope_tables(seq_len):
    rows = seq_len // GRID_W
    row_id = jnp.broadcast_to(jnp.arange(rows)[:, None], (rows, GRID_W)).reshape(-1).astype(jnp.float32)
    col_id = jnp.broadcast_to(jnp.arange(GRID_W)[None, :], (rows, GRID_W)).reshape(-1).astype(jnp.float32)
    axis_dim = HEAD_DIM // 2
    inv_freq = jnp.power(jnp.float32(ROPE_THETA), -jnp.arange(0, axis_dim, 2, dtype=jnp.float32) / axis_dim)
    ang_r = row_id[:, None] * inv_freq[None, :]
    ang_c = col_id[:, None] * inv_freq[None, :]
    ang = jnp.concatenate([ang_r, ang_r, ang_c, ang_c], axis=-1)
    return jnp.cos(ang), jnp.sin(ang)


def rotate_half(a):
    a1, a2 = jnp.split(a, 2, axis=-1)
    return jnp.concatenate([-a2, a1], axis=-1)


def apply_axial_rope(x, cos, sin):
    xr, xc = jnp.split(x, 2, axis=-1)
    rot = jnp.concatenate([rotate_half(xr), rotate_half(xc)], axis=-1)
    return x * cos + rot * sin


def block_attention(q, k, v):
    B, S = q.shape[0], q.shape[1]
    groups = N_Q_HEADS // N_KV_HEADS
    nb = S // Q_BLOCK
    qb = q.reshape(B, nb, Q_BLOCK, N_KV_HEADS, groups, HEAD_DIM).transpose(1, 0, 2, 3, 4, 5)
    scale = jnp.float32(HEAD_DIM ** -0.5)

    def one_block(qi):
        s = jnp.einsum('bqkgd,bskd->bkgqs', qi, k, preferred_element_type=jnp.float32) * scale
        p = jax.nn.softmax(s, axis=-1)
        return jnp.einsum('bkgqs,bskd->bqkgd', p.astype(v.dtype), v)

    ob = lax.map(one_block, qb)
    return ob.transpose(1, 0, 2, 3, 4, 5).reshape(B, S, ATTN_WIDTH)


def chunked_token_mlp(z, g_v, w_s, b_s):
    B, S = z.shape[0], z.shape[1]
    nc = S // CHUNK
    z = jax.nn.gelu(z)
    u, v = jnp.split(z, 2, axis=-1)
    v = rmsnorm(v.reshape(B, S, N_TOK_HEADS, HEAD_DIM), g_v)
    v = v.reshape(B, nc, CHUNK, N_TOK_HEADS, HEAD_DIM)
    mixed = jnp.einsum('hpq,bcqhd->bcphd', w_s, v) + b_s.T[None, None, :, :, None]
    out = u.reshape(B, nc, CHUNK, N_TOK_HEADS, HEAD_DIM) * mixed
    return out.reshape(B, S, TOK_WIDTH)


def conv_glu_ffn(h, w_up, w_conv, b_conv, w_down):
    z = h @ w_up
    zp = jnp.pad(z, ((0, 0), (1, 1), (0, 0)))
    z = zp[:, :-2] * w_conv[0] + zp[:, 1:-1] * w_conv[1] + zp[:, 2:] * w_conv[2] + b_conv
    a, b = jnp.split(z, 2, axis=-1)
    return (jax.nn.silu(a) * b) @ w_down


def setup_inputs(seed: int = 0) -> dict:
    key = jax.random.key(seed)
    ks = jax.random.split(key, 24)
    f32 = jnp.float32
    L, D = DEPTH, D_MODEL

    def nrm(k, shape, scale):
        return jax.random.normal(k, shape, f32) * scale

    def gain(k, shape):
        return 1.0 + 0.05 * jax.random.normal(k, shape, f32)

    return {
        "x": nrm(ks[0], (BATCH, SEQ, D), 1.0),
        "c": nrm(ks[1], (BATCH, D), 1.0),
        "w_ada": nrm(ks[2], (L, D, N_MOD * D), 0.5 * D ** -0.5),
        "b_ada": nrm(ks[3], (L, N_MOD * D), 0.01),
        "g_norm1": gain(ks[4], (L, D)),
        "w_in": nrm(ks[5], (L, D, N_IN), D ** -0.5),
        "g_q": gain(ks[6], (L, HEAD_DIM)),
        "g_k": gain(ks[7], (L, HEAD_DIM)),
        "g_tok_v": gain(ks[8], (L, HEAD_DIM)),
        "w_s": nrm(ks[9], (L, N_TOK_HEADS, CHUNK, CHUNK), CHUNK ** -0.5),
        "b_s": 1.0 + nrm(ks[10], (L, N_TOK_HEADS, CHUNK), 0.01),
        "g_attn_out": gain(ks[11], (L, ATTN_WIDTH)),
        "g_tok_out": gain(ks[12], (L, TOK_WIDTH)),
        "w_out": nrm(ks[13], (L, MIX_WIDTH, D), MIX_WIDTH ** -0.5),
        "g_norm2": gain(ks[14], (L, D)),
        "w_up": nrm(ks[15], (L, D, 2 * D_FF), D ** -0.5),
        "w_conv": nrm(ks[16], (L, CONV_W, 2 * D_FF), CONV_W ** -0.5),
        "b_conv": nrm(ks[17], (L, 2 * D_FF), 0.01),
        "w_down": nrm(ks[18], (L, D_FF, D), D_FF ** -0.5),
        "g_final": gain(ks[19], (D,)),
    }


def reference(x, c, w_ada, b_ada, g_norm1, w_in, g_q, g_k, g_tok_v, w_s, b_s,
              g_attn_out, g_tok_out, w_out, g_norm2, w_up, w_conv, b_conv, w_down, g_final):
    B, S, _ = x.shape
    cos, sin = axial_rope_tables(S)
    cos = cos.astype(x.dtype)[None, :, None, :]
    sin = sin.astype(x.dtype)[None, :, None, :]
    mod_all = jnp.einsum('bd,ldm->lbm', jax.nn.silu(c), w_ada) + b_ada[:, None, :]

    for l in range(DEPTH):
        sh1, sc1, gt1, sh2, sc2, gt2 = jnp.split(mod_all[l][:, None, :], N_MOD, axis=-1)

        h = rmsnorm(x, g_norm1[l]) * (1.0 + sc1) + sh1
        proj = h @ w_in[l]
        q, k, v, z = jnp.split(
            proj, [ATTN_WIDTH, ATTN_WIDTH + KV_WIDTH, ATTN_WIDTH + 2 * KV_WIDTH], axis=-1)
        q = rmsnorm(q.reshape(B, S, N_Q_HEADS, HEAD_DIM), g_q[l])
        k = rmsnorm(k.reshape(B, S, N_KV_HEADS, HEAD_DIM), g_k[l])
        q = apply_axial_rope(q, cos, sin)
        k = apply_axial_rope(k, cos, sin)
        v = v.reshape(B, S, N_KV_HEADS, HEAD_DIM)
        attn = block_attention(q, k, v)
        tok = chunked_token_mlp(z, g_tok_v[l], w_s[l], b_s[l])
        mixed = jnp.concatenate(
            [rmsnorm(attn, g_attn_out[l]), rmsnorm(tok, g_tok_out[l])], axis=-1)
        x = x + gt1 * (mixed @ w_out[l])

        h = rmsnorm(x, g_norm2[l]) * (1.0 + sc2) + sh2
        x = x + gt2 * conv_glu_ffn(h, w_up[l], w_conv[l], b_conv[l], w_down[l])

    return rmsnorm(x, g_final)
```

```python
import functools
import math

import jax
import jax.numpy as jnp
import numpy as np
from jax import lax
from jax.experimental import pallas as pl
from jax.experimental.pallas import tpu as pltpu

HEAD_DIM = 64
N_Q_HEADS = 8
N_KV_HEADS = 2
Q_PER_KV = N_Q_HEADS // N_KV_HEADS
ATTN_WIDTH = N_Q_HEADS * HEAD_DIM
KV_WIDTH = N_KV_HEADS * HEAD_DIM
TOK_WIDTH = 512
N_TOK_HEADS = TOK_WIDTH // HEAD_DIM
CHUNK = 128
GRID_W = 64
ROPE_THETA = 10000.0
EPS = 1e-6
LANES = 128
BF16_SUBLANES = 16
VMEM_LIMIT_BYTES = 56 * 1024 * 1024

Q_SCALE = (HEAD_DIM ** -0.5) * math.log2(math.e)

F32 = jnp.float32
BF16 = jnp.bfloat16


def _dot(a, b):
    return jnp.dot(a, b, preferred_element_type=F32)


def _head_mean_square(y, pm):
    yy = y * y
    hi = yy.astype(BF16)
    lo = (yy - hi.astype(F32)).astype(BF16)
    return _dot(hi, pm) + _dot(lo, pm)


def _rope(y, cos, sin_a, sin_b):
    w = y.shape[-1]
    outs = []
    for c in range(w // LANES):
        yc = y[:, c * LANES:(c + 1) * LANES]
        up = pltpu.roll(yc, LANES - HEAD_DIM // 4, 1)
        dn = pltpu.roll(yc, HEAD_DIM // 4, 1)
        outs.append(yc * cos + up * sin_a + dn * sin_b)
    return outs[0] if len(outs) == 1 else jnp.concatenate(outs, axis=-1)


def _adaln_kernel(c_ref, w_ref, b_ref, o_ref):
    c = c_ref[...]
    a = c / (1.0 + jnp.exp(-c))
    o_ref[...] = jnp.dot(a, w_ref[...], preferred_element_type=F32,
                         precision=lax.Precision.HIGHEST) + b_ref[...]


def _adaln(c8, w_ada, b_ada):
    d, n = w_ada.shape
    tn = 1536
    return pl.pallas_call(
        _adaln_kernel,
        out_shape=jax.ShapeDtypeStruct((c8.shape[0], n), F32),
        grid=(n // tn,),
        in_specs=[pl.BlockSpec((c8.shape[0], d), lambda j: (0, 0)),
                  pl.BlockSpec((d, tn), lambda j: (0, j)),
                  pl.BlockSpec((1, tn), lambda j: (0, j))],
        out_specs=pl.BlockSpec((c8.shape[0], tn), lambda j: (0, j)),
        compiler_params=pltpu.CompilerParams(
            dimension_semantics=("arbitrary",), vmem_limit_bytes=VMEM_LIMIT_BYTES),
        name="adaln",
    )(c8, w_ada, b_ada)


def _inproj_kernel(x_ref, sh_ref, sc_ref, g1_ref, w_ref, cos_ref, sa_ref, sb_ref,
                   gq_ref, gk_ref, gv_ref, pm_ref, ws_ref, bs_ref, gto_ref,
                   qT_ref, k_ref, vT_ref, tok_ref):
    tm = x_ref.shape[0]
    x = x_ref[...]
    h = x * lax.rsqrt(jnp.mean(x * x, axis=-1, keepdims=True) + EPS) * g1_ref[...]
    h = h * (1.0 + sc_ref[...]) + sh_ref[...]
    proj = _dot(h.astype(BF16), w_ref[...])

    cos, sa, sb = cos_ref[...], sa_ref[...], sb_ref[...]
    pm = pm_ref[...]

    q = proj[:, :ATTN_WIDTH]
    ms = jnp.concatenate([_head_mean_square(q[:, c:c + 256], pm)
                          for c in range(0, ATTN_WIDTH, 256)], axis=-1)
    q = q * lax.rsqrt(ms + EPS) * gq_ref[...]
    q = _rope(q, cos, sa, sb) * Q_SCALE
    qT_ref[...] = q.T.astype(BF16)

    k = proj[:, ATTN_WIDTH:ATTN_WIDTH + KV_WIDTH]
    ms = _head_mean_square(k, pm[:KV_WIDTH, :KV_WIDTH])
    k = k * lax.rsqrt(ms + EPS) * gk_ref[...]
    k = _rope(k, cos, sa, sb)
    lane = lax.broadcasted_iota(jnp.int32, k.shape, 1)
    k_ref[0] = jnp.where(lane < HEAD_DIM, k, 0.0).astype(BF16)
    k_ref[1] = jnp.where(lane >= HEAD_DIM, k, 0.0).astype(BF16)

    v = proj[:, ATTN_WIDTH + KV_WIDTH:ATTN_WIDTH + 2 * KV_WIDTH]
    vT_ref[...] = v.T.astype(BF16)

    z0 = ATTN_WIDTH + 2 * KV_WIDTH
    z = jax.nn.gelu(proj[:, z0:], approximate=True)
    u = z[:, :TOK_WIDTH]
    vv = z[:, TOK_WIDTH:]
    ms = jnp.concatenate([_head_mean_square(vv[:, c:c + 256], pm)
                          for c in range(0, TOK_WIDTH, 256)], axis=-1)
    vn = (vv * lax.rsqrt(ms + EPS) * gv_ref[...]).astype(BF16)

    lane_b = lax.broadcasted_iota(jnp.int32, (CHUNK, LANES), 1)
    zero = jnp.zeros((CHUNK, LANES), BF16)
    rows = []
    for c in range(tm // CHUNK):
        cols = []
        for jp in range(N_TOK_HEADS // 2):
            vt = vn[c * CHUNK:(c + 1) * CHUNK, jp * LANES:(jp + 1) * LANES]
            rhs = jnp.concatenate([jnp.where(lane_b < HEAD_DIM, vt, zero),
                                   jnp.where(lane_b >= HEAD_DIM, vt, zero)], axis=0)
            cols.append(_dot(ws_ref[jp], rhs) + bs_ref[jp])
        rows.append(jnp.concatenate(cols, axis=-1))
    tok = u * jnp.concatenate(rows, axis=0)
    tok = tok * lax.rsqrt(jnp.mean(tok * tok, axis=-1, keepdims=True) + EPS) * gto_ref[...]
    tok_ref[...] = tok.astype(BF16)


def _inproj(x2, sh1, sc1, g1, w_in, cos, sin_a, sin_b, gq, gk, gv, pm, ws2, bs2, gto,
            *, batch, seq, tm):
    n, d = x2.shape
    tpb = seq // tm
    n_in = w_in.shape[1]
    const = lambda i: (0, 0)
    const3 = lambda i: (0, 0, 0)
    per_batch = lambda i: (i // tpb, 0, 0)
    return pl.pallas_call(
        _inproj_kernel,
        out_shape=(jax.ShapeDtypeStruct((batch, ATTN_WIDTH, seq), BF16),
                   jax.ShapeDtypeStruct((batch, N_KV_HEADS, seq, KV_WIDTH), BF16),
                   jax.ShapeDtypeStruct((batch, KV_WIDTH, seq), BF16),
                   jax.ShapeDtypeStruct((n, TOK_WIDTH), BF16)),
        grid=(n // tm,),
        in_specs=[pl.BlockSpec((tm, d), lambda i: (i, 0)),
                  pl.BlockSpec((None, 1, d), per_batch),
                  pl.BlockSpec((None, 1, d), per_batch),
                  pl.BlockSpec((1, d), const),
                  pl.BlockSpec((d, n_in), const, pipeline_mode=pl.Buffered(1)),
                  pl.BlockSpec((tm, LANES), lambda i: (i % tpb, 0)),
                  pl.BlockSpec((tm, LANES), lambda i: (i % tpb, 0)),
                  pl.BlockSpec((tm, LANES), lambda i: (i % tpb, 0)),
                  pl.BlockSpec((1, ATTN_WIDTH), const),
                  pl.BlockSpec((1, KV_WIDTH), const),
                  pl.BlockSpec((1, TOK_WIDTH), const),
                  pl.BlockSpec((256, 256), const),
                  pl.BlockSpec(ws2.shape, const3),
                  pl.BlockSpec(bs2.shape, const3),
                  pl.BlockSpec((1, TOK_WIDTH), const)],
        out_specs=(pl.BlockSpec((None, ATTN_WIDTH, tm), lambda i: (i // tpb, 0, i % tpb)),
                   pl.BlockSpec((None, N_KV_HEADS, tm, KV_WIDTH),
                                lambda i: (i // tpb, 0, i % tpb, 0)),
                   pl.BlockSpec((None, KV_WIDTH, tm), lambda i: (i // tpb, 0, i % tpb)),
                   pl.BlockSpec((tm, TOK_WIDTH), lambda i: (i, 0))),
        compiler_params=pltpu.CompilerParams(
            dimension_semantics=("arbitrary",), vmem_limit_bytes=VMEM_LIMIT_BYTES),
        name="inproj",
    )(x2, sh1, sc1, g1, w_in, cos, sin_a, sin_b, gq, gk, gv, pm, ws2, bs2, gto)


def _attn_kernel(qT_ref, k_ref, vT_ref, g_ref, o_ref, m_ref, l_ref, acc_ref, *, tk):
    seq = k_ref.shape[1]
    m_ref[...] = jnp.full(m_ref.shape, -1e30, F32)
    l_ref[...] = jnp.zeros(l_ref.shape, F32)
    acc_ref[...] = jnp.zeros(acc_ref.shape, F32)

    def body(j, carry):
        off = pl.multiple_of(j * tk, tk)
        for g in range(N_KV_HEADS):
            kg = k_ref[g, pl.ds(off, tk), :]
            vg = vT_ref[g * HEAD_DIM:(g + 1) * HEAD_DIM, pl.ds(off, tk)]
            for hh in range(Q_PER_KV):
                h = g * Q_PER_KV + hh
                r0 = h * HEAD_DIM - g * HEAD_DIM
                s = _dot(kg, qT_ref[r0:r0 + KV_WIDTH, :])
                m_prev = m_ref[h:h + 1, :]
                m_new = jnp.maximum(m_prev, jnp.max(s, axis=0, keepdims=True))
                alpha = jnp.exp2(m_prev - m_new)
                p = jnp.exp2(s - m_new)
                l_ref[h:h + 1, :] = alpha * l_ref[h:h + 1, :] + jnp.sum(p, axis=0, keepdims=True)
                m_ref[h:h + 1, :] = m_new
                rows = slice(h * HEAD_DIM, (h + 1) * HEAD_DIM)
                acc_ref[rows, :] = alpha * acc_ref[rows, :] + _dot(vg, p.astype(BF16))
        return carry

    lax.fori_loop(0, seq // tk, body, 0)

    inv = 1.0 / l_ref[...]
    for h in range(N_Q_HEADS):
        rows = slice(h * HEAD_DIM, (h + 1) * HEAD_DIM)
        acc_ref[rows, :] = acc_ref[rows, :] * inv[h:h + 1, :]
    o = acc_ref[...].T
    o = o * lax.rsqrt(jnp.mean(o * o, axis=-1, keepdims=True) + EPS) * g_ref[...]
    o_ref[...] = o.astype(BF16)


def _attention(qT, k2, vT, g_attn, *, tq, tk):
    batch, _, seq = qT.shape
    return pl.pallas_call(
        functools.partial(_attn_kernel, tk=tk),
        out_shape=jax.ShapeDtypeStruct((batch, seq, ATTN_WIDTH), BF16),
        grid=(batch, seq // tq),
        in_specs=[pl.BlockSpec((None, ATTN_WIDTH, tq), lambda b, i: (b, 0, i)),
                  pl.BlockSpec((None, N_KV_HEADS, seq, KV_WIDTH), lambda b, i: (b, 0, 0, 0)),
                  pl.BlockSpec((None, KV_WIDTH, seq), lambda b, i: (b, 0, 0)),
                  pl.BlockSpec((1, ATTN_WIDTH), lambda b, i: (0, 0))],
        out_specs=pl.BlockSpec((None, tq, ATTN_WIDTH), lambda b, i: (b, i, 0)),
        scratch_shapes=[pltpu.VMEM((N_Q_HEADS, tq), F32),
                        pltpu.VMEM((N_Q_HEADS, tq), F32),
                        pltpu.VMEM((ATTN_WIDTH, tq), F32)],
        compiler_params=pltpu.CompilerParams(
            dimension_semantics=("arbitrary", "arbitrary"), vmem_limit_bytes=VMEM_LIMIT_BYTES),
        name="attention",
    )(qT, k2, vT, g_attn)


def _outproj_kernel(a_ref, t_ref, x_ref, wa_ref, wt_ref, gt_ref, g2_ref, sc_ref, sh_ref,
                    x1_ref, h2_ref):
    y = _dot(a_ref[...], wa_ref[...]) + _dot(t_ref[...], wt_ref[...])
    x1 = x_ref[...] + gt_ref[...] * y
    x1_ref[...] = x1
    h = x1 * lax.rsqrt(jnp.mean(x1 * x1, axis=-1, keepdims=True) + EPS) * g2_ref[...]
    h2_ref[...] = (h * (1.0 + sc_ref[...]) + sh_ref[...]).astype(BF16)


def _outproj(attn, tok, x2, w_oa, w_ot, gt1, g2, sc2, sh2, *, seq, tm):
    n, d = x2.shape
    tpb = seq // tm
    const = lambda i: (0, 0)
    per_batch = lambda i: (i // tpb, 0, 0)
    row = lambda i: (i, 0)
    return pl.pallas_call(
        _outproj_kernel,
        out_shape=(jax.ShapeDtypeStruct((n, d), F32), jax.ShapeDtypeStruct((n, d), BF16)),
        grid=(n // tm,),
        in_specs=[pl.BlockSpec((tm, ATTN_WIDTH), row),
                  pl.BlockSpec((tm, TOK_WIDTH), row),
                  pl.BlockSpec((tm, d), row),
                  pl.BlockSpec(w_oa.shape, const, pipeline_mode=pl.Buffered(1)),
                  pl.BlockSpec(w_ot.shape, const, pipeline_mode=pl.Buffered(1)),
                  pl.BlockSpec((None, 1, d), per_batch),
                  pl.BlockSpec((1, d), const),
                  pl.BlockSpec((None, 1, d), per_batch),
                  pl.BlockSpec((None, 1, d), per_batch)],
        out_specs=(pl.BlockSpec((tm, d), row), pl.BlockSpec((tm, d), row)),
        compiler_params=pltpu.CompilerParams(
            dimension_semantics=("arbitrary",), vmem_limit_bytes=VMEM_LIMIT_BYTES),
        name="outproj",
    )(attn, tok, x2, w_oa, w_ot, gt1, g2, sc2, sh2)


def _ffn_kernel(hp_ref, hm_ref, hn_ref, x1_ref, wa_ref, wb_ref, wca_ref, wcb_ref,
                bca_ref, bcb_ref, wd_ref, gt_ref, gf_ref, o_ref, hbuf_ref, acc_ref,
                *, tpb, tn):
    tm = hm_ref.shape[0]
    halo = hp_ref.shape[0]
    d_ff = wa_ref.shape[1]
    i = pl.program_id(0)
    first = (i % tpb) == 0
    last = (i % tpb) == tpb - 1
    hbuf_ref[0:halo, :] = jnp.where(first, jnp.zeros_like(hp_ref[...]), hp_ref[...])
    hbuf_ref[halo:halo + tm, :] = hm_ref[...]
    hbuf_ref[halo + tm:, :] = jnp.where(last, jnp.zeros_like(hn_ref[...]), hn_ref[...])
    acc_ref[...] = jnp.zeros(acc_ref.shape, F32)
    rows = tm + 2 * halo

    def conv(z, wc_ref, bc_ref, cols):
        z_prev = pltpu.roll(z, 1, 0)
        z_next = pltpu.roll(z, rows - 1, 0)
        c = (z_prev * wc_ref[0:1, cols] + z * wc_ref[1:2, cols]
             + z_next * wc_ref[2:3, cols] + bc_ref[:, cols])
        return c[halo:halo + tm, :]

    def body(j, carry):
        cols = pl.ds(pl.multiple_of(j * tn, tn), tn)
        hb = hbuf_ref[...]
        a = conv(_dot(hb, wa_ref[:, cols]), wca_ref, bca_ref, cols)
        b = conv(_dot(hb, wb_ref[:, cols]), wcb_ref, bcb_ref, cols)
        gated = (a / (1.0 + jnp.exp(-a)) * b).astype(BF16)
        acc_ref[...] += _dot(gated, wd_ref[cols, :])
        return carry

    lax.fori_loop(0, d_ff // tn, body, 0)

    x2 = x1_ref[...] + gt_ref[...] * acc_ref[...]
    o_ref[...] = x2 * lax.rsqrt(jnp.mean(x2 * x2, axis=-1, keepdims=True) + EPS) * gf_ref[...]


def _ffn(h2, x1, w_ua, w_ub, wc_a, wc_b, bc_a, bc_b, w_down, gt2, g_final, *, seq, tm, tn):
    n, d = x1.shape
    d_ff = w_down.shape[0]
    halo = BF16_SUBLANES
    tpb = seq // tm
    hb = tm // halo
    last_blk = n // halo - 1
    const = lambda i: (0, 0)
    row = lambda i: (i, 0)
    return pl.pallas_call(
        functools.partial(_ffn_kernel, tpb=tpb, tn=tn),
        out_shape=jax.ShapeDtypeStruct((n, d), F32),
        grid=(n // tm,),
        in_specs=[pl.BlockSpec((halo, d), lambda i: (jnp.maximum(i * hb - 1, 0), 0)),
                  pl.BlockSpec((tm, d), row),
                  pl.BlockSpec((halo, d), lambda i: (jnp.minimum((i + 1) * hb, last_blk), 0)),
                  pl.BlockSpec((tm, d), row),
                  pl.BlockSpec((d, d_ff), const, pipeline_mode=pl.Buffered(1)),
                  pl.BlockSpec((d, d_ff), const, pipeline_mode=pl.Buffered(1)),
                  pl.BlockSpec((3, d_ff), const),
                  pl.BlockSpec((3, d_ff), const),
                  pl.BlockSpec((1, d_ff), const),
                  pl.BlockSpec((1, d_ff), const),
                  pl.BlockSpec((d_ff, d), const, pipeline_mode=pl.Buffered(1)),
                  pl.BlockSpec((None, 1, d), lambda i: (i // tpb, 0, 0)),
                  pl.BlockSpec((1, d), const)],
        out_specs=pl.BlockSpec((tm, d), row),
        scratch_shapes=[pltpu.VMEM((tm + 2 * halo, d), BF16),
                        pltpu.VMEM((tm, d), F32)],
        compiler_params=pltpu.CompilerParams(
            dimension_semantics=("arbitrary",), vmem_limit_bytes=VMEM_LIMIT_BYTES),
        name="ffn",
    )(h2, h2, h2, x1, w_ua, w_ub, wc_a, wc_b, bc_a, bc_b, w_down, gt2, g_final)


def _rope_tables(seq):
    rows = seq // GRID_W
    row_id = jnp.broadcast_to(jnp.arange(rows)[:, None], (rows, GRID_W)).reshape(-1).astype(F32)
    col_id = jnp.broadcast_to(jnp.arange(GRID_W)[None, :], (rows, GRID_W)).reshape(-1).astype(F32)
    axis_dim = HEAD_DIM // 2
    inv_freq = jnp.power(jnp.float32(ROPE_THETA),
                         -jnp.arange(0, axis_dim, 2, dtype=F32) / axis_dim)
    ang_r = row_id[:, None] * inv_freq[None, :]
    ang_c = col_id[:, None] * inv_freq[None, :]
    ang = jnp.concatenate([ang_r, ang_r, ang_c, ang_c], axis=-1)
    cos, sin = jnp.cos(ang), jnp.sin(ang)
    first = (np.arange(HEAD_DIM) % (HEAD_DIM // 2)) < HEAD_DIM // 4
    sin_a = jnp.where(first[None, :], -sin, 0.0)
    sin_b = jnp.where(first[None, :], 0.0, sin)
    rep = LANES // HEAD_DIM
    return jnp.tile(cos, (1, rep)), jnp.tile(sin_a, (1, rep)), jnp.tile(sin_b, (1, rep))


def _block_mean_matrix(width):
    idx = np.arange(width) // HEAD_DIM
    return jnp.asarray((idx[:, None] == idx[None, :]).astype(np.float32) / HEAD_DIM, dtype=BF16)


def kernel(x, c, w_ada, b_ada, g_norm1, w_in, g_q, g_k, g_tok_v, w_s, b_s, g_attn_out,
           g_tok_out, w_out, g_norm2, w_up, w_conv, b_conv, w_down, g_final):
    batch, seq, d = x.shape
    depth = w_ada.shape[0]
    assert depth == 1, "the final RMSNorm is fused into the (single) layer's FFN kernel"
    d_ff = w_down.shape[1]
    n = batch * seq
    tm = 512
    cos, sin_a, sin_b = _rope_tables(seq)
    pm = _block_mean_matrix(256)

    c8 = jnp.zeros((8, d), F32).at[:batch].set(c)
    xf = x.reshape(n, d)
    for l in range(depth):
        mod = _adaln(c8, w_ada[l], b_ada[l][None, :])[:batch]
        sh1, sc1, gt1, sh2, sc2, gt2 = [m[:, None, :] for m in jnp.split(mod, 6, axis=-1)]

        tile_h = lambda g, w: jnp.tile(g, w // HEAD_DIM)[None, :]
        ws2 = jnp.concatenate([w_s[l][0::2], w_s[l][1::2]], axis=-1).astype(BF16)
        bs2 = jnp.repeat(b_s[l].reshape(N_TOK_HEADS // 2, 2, CHUNK), HEAD_DIM, axis=1)
        bs2 = jnp.swapaxes(bs2, 1, 2)

        qT, k2, vT, tok = _inproj(
            xf, sh1, sc1, g_norm1[l][None, :], w_in[l].astype(BF16), cos, sin_a, sin_b,
            tile_h(g_q[l], ATTN_WIDTH), tile_h(g_k[l], KV_WIDTH), tile_h(g_tok_v[l], TOK_WIDTH),
            pm, ws2, bs2, g_tok_out[l][None, :], batch=batch, seq=seq, tm=tm)

        attn = _attention(qT, k2, vT, g_attn_out[l][None, :], tq=512, tk=256)

        w_o = w_out[l].astype(BF16)
        x1, h2 = _outproj(attn.reshape(n, ATTN_WIDTH), tok, xf, w_o[:ATTN_WIDTH], w_o[ATTN_WIDTH:],
                          gt1, g_norm2[l][None, :], sc2, sh2, seq=seq, tm=tm)

        w_u = w_up[l].astype(BF16)
        xf = _ffn(h2, x1, w_u[:, :d_ff], w_u[:, d_ff:], w_conv[l][:, :d_ff], w_conv[l][:, d_ff:],
                  b_conv[l][None, :d_ff], b_conv[l][None, d_ff:], w_down[l].astype(BF16),
                  gt2, g_final[None, :], seq=seq, tm=tm, tn=256)
    return xf.reshape(batch, seq, d)
```

```python
import functools
import math

import jax
import jax.numpy as jnp
import numpy as np
from jax import lax
from jax.experimental import pallas as pl
from jax.experimental.pallas import tpu as pltpu

HEAD_DIM = 64
N_Q_HEADS = 8
N_KV_HEADS = 2
Q_PER_KV = N_Q_HEADS // N_KV_HEADS
ATTN_WIDTH = N_Q_HEADS * HEAD_DIM
KV_WIDTH = N_KV_HEADS * HEAD_DIM
TOK_WIDTH = 512
N_TOK_HEADS = TOK_WIDTH // HEAD_DIM
CHUNK = 128
GRID_W = 64
ROPE_THETA = 10000.0
EPS = 1e-6
LANES = 128
BF16_SUBLANES = 16
VMEM_LIMIT_BYTES = 56 * 1024 * 1024

Q_SCALE = (HEAD_DIM ** -0.5) * math.log2(math.e)
MAX_STATIC_SHIFT = 60.0
BF16_ROUNDING_MARGIN = (1.0 + 2.0 ** -7) ** 2

F32 = jnp.float32
BF16 = jnp.bfloat16


def _dot(a, b):
    return jnp.dot(a, b, preferred_element_type=F32)


def _head_mean_square(y, pm):
    yy = y * y
    hi = yy.astype(BF16)
    lo = (yy - hi.astype(F32)).astype(BF16)
    return _dot(hi, pm) + _dot(lo, pm)


def _rope(y, cos, sin_a, sin_b):
    w = y.shape[-1]
    outs = []
    for c in range(w // LANES):
        yc = y[:, c * LANES:(c + 1) * LANES]
        up = pltpu.roll(yc, LANES - HEAD_DIM // 4, 1)
        dn = pltpu.roll(yc, HEAD_DIM // 4, 1)
        outs.append(yc * cos + up * sin_a + dn * sin_b)
    return outs[0] if len(outs) == 1 else jnp.concatenate(outs, axis=-1)


def _adaln_kernel(c_ref, w_ref, b_ref, o_ref):
    c = c_ref[...]
    a = c / (1.0 + jnp.exp(-c))
    o_ref[...] = jnp.dot(a, w_ref[...], preferred_element_type=F32,
                         precision=lax.Precision.HIGHEST) + b_ref[...]


def _adaln(c8, w_ada, b_ada):
    d, n = w_ada.shape
    tn = 1536
    return pl.pallas_call(
        _adaln_kernel,
        out_shape=jax.ShapeDtypeStruct((c8.shape[0], n), F32),
        grid=(n // tn,),
        in_specs=[pl.BlockSpec((c8.shape[0], d), lambda j: (0, 0)),
                  pl.BlockSpec((d, tn), lambda j: (0, j)),
                  pl.BlockSpec((1, tn), lambda j: (0, j))],
        out_specs=pl.BlockSpec((c8.shape[0], tn), lambda j: (0, j)),
        compiler_params=pltpu.CompilerParams(
            dimension_semantics=("arbitrary",), vmem_limit_bytes=VMEM_LIMIT_BYTES),
        name="adaln",
    )(c8, w_ada, b_ada)


def _inproj_kernel(x_ref, sh_ref, sc_ref, g1_ref, w_ref, cos_ref, sa_ref, sb_ref,
                   gq_ref, gk_ref, gv_ref, pm_ref, ws_ref, bs_ref, gto_ref,
                   qT_ref, k_ref, vT_ref, tok_ref):
    tm = x_ref.shape[0]
    x = x_ref[...]
    h = x * lax.rsqrt(jnp.mean(x * x, axis=-1, keepdims=True) + EPS) * g1_ref[...]
    h = h * (1.0 + sc_ref[...]) + sh_ref[...]
    proj = _dot(h.astype(BF16), w_ref[...])

    cos, sa, sb = cos_ref[...], sa_ref[...], sb_ref[...]
    pm = pm_ref[...]

    q = proj[:, :ATTN_WIDTH]
    ms = jnp.concatenate([_head_mean_square(q[:, c:c + 256], pm)
                          for c in range(0, ATTN_WIDTH, 256)], axis=-1)
    q = q * lax.rsqrt(ms + EPS) * gq_ref[...]
    q = _rope(q, cos, sa, sb) * Q_SCALE
    qT_ref[...] = q.T.astype(BF16)

    k = proj[:, ATTN_WIDTH:ATTN_WIDTH + KV_WIDTH]
    ms = _head_mean_square(k, pm[:KV_WIDTH, :KV_WIDTH])
    k = k * lax.rsqrt(ms + EPS) * gk_ref[...]
    k = _rope(k, cos, sa, sb)
    lane = lax.broadcasted_iota(jnp.int32, k.shape, 1)
    k_ref[0] = jnp.where(lane < HEAD_DIM, k, 0.0).astype(BF16)
    k_ref[1] = jnp.where(lane >= HEAD_DIM, k, 0.0).astype(BF16)

    v = proj[:, ATTN_WIDTH + KV_WIDTH:ATTN_WIDTH + 2 * KV_WIDTH]
    vT_ref[...] = v.T.astype(BF16)

    z0 = ATTN_WIDTH + 2 * KV_WIDTH
    z = jax.nn.gelu(proj[:, z0:], approximate=True)
    u = z[:, :TOK_WIDTH]
    vv = z[:, TOK_WIDTH:]
    ms = jnp.concatenate([_head_mean_square(vv[:, c:c + 256], pm)
                          for c in range(0, TOK_WIDTH, 256)], axis=-1)
    vn = (vv * lax.rsqrt(ms + EPS) * gv_ref[...]).astype(BF16)

    lane_b = lax.broadcasted_iota(jnp.int32, (CHUNK, LANES), 1)
    zero = jnp.zeros((CHUNK, LANES), BF16)
    rows = []
    for c in range(tm // CHUNK):
        cols = []
        for jp in range(N_TOK_HEADS // 2):
            vt = vn[c * CHUNK:(c + 1) * CHUNK, jp * LANES:(jp + 1) * LANES]
            rhs = jnp.concatenate([jnp.where(lane_b < HEAD_DIM, vt, zero),
                                   jnp.where(lane_b >= HEAD_DIM, vt, zero)], axis=0)
            cols.append(_dot(ws_ref[jp], rhs) + bs_ref[jp])
        rows.append(jnp.concatenate(cols, axis=-1))
    tok = u * jnp.concatenate(rows, axis=0)
    tok = tok * lax.rsqrt(jnp.mean(tok * tok, axis=-1, keepdims=True) + EPS) * gto_ref[...]
    tok_ref[...] = tok.astype(BF16)


def _inproj(x2, sh1, sc1, g1, w_in, cos, sin_a, sin_b, gq, gk, gv, pm, ws2, bs2, gto,
            *, batch, seq, tm):
    n, d = x2.shape
    tpb = seq // tm
    n_in = w_in.shape[1]
    const = lambda i: (0, 0)
    const3 = lambda i: (0, 0, 0)
    per_batch = lambda i: (i // tpb, 0, 0)
    return pl.pallas_call(
        _inproj_kernel,
        out_shape=(jax.ShapeDtypeStruct((batch, ATTN_WIDTH, seq), BF16),
                   jax.ShapeDtypeStruct((batch, N_KV_HEADS, seq, KV_WIDTH), BF16),
                   jax.ShapeDtypeStruct((batch, KV_WIDTH, seq), BF16),
                   jax.ShapeDtypeStruct((n, TOK_WIDTH), BF16)),
        grid=(n // tm,),
        in_specs=[pl.BlockSpec((tm, d), lambda i: (i, 0)),
                  pl.BlockSpec((None, 1, d), per_batch),
                  pl.BlockSpec((None, 1, d), per_batch),
                  pl.BlockSpec((1, d), const),
                  pl.BlockSpec((d, n_in), const, pipeline_mode=pl.Buffered(1)),
                  pl.BlockSpec((tm, LANES), lambda i: (i % tpb, 0)),
                  pl.BlockSpec((tm, LANES), lambda i: (i % tpb, 0)),
                  pl.BlockSpec((tm, LANES), lambda i: (i % tpb, 0)),
                  pl.BlockSpec((1, ATTN_WIDTH), const),
                  pl.BlockSpec((1, KV_WIDTH), const),
                  pl.BlockSpec((1, TOK_WIDTH), const),
                  pl.BlockSpec((256, 256), const),
                  pl.BlockSpec(ws2.shape, const3),
                  pl.BlockSpec(bs2.shape, const3),
                  pl.BlockSpec((1, TOK_WIDTH), const)],
        out_specs=(pl.BlockSpec((None, ATTN_WIDTH, tm), lambda i: (i // tpb, 0, i % tpb)),
                   pl.BlockSpec((None, N_KV_HEADS, tm, KV_WIDTH),
                                lambda i: (i // tpb, 0, i % tpb, 0)),
                   pl.BlockSpec((None, KV_WIDTH, tm), lambda i: (i // tpb, 0, i % tpb)),
                   pl.BlockSpec((tm, TOK_WIDTH), lambda i: (i, 0))),
        compiler_params=pltpu.CompilerParams(
            dimension_semantics=("arbitrary",), vmem_limit_bytes=VMEM_LIMIT_BYTES),
        name="inproj",
    )(x2, sh1, sc1, g1, w_in, cos, sin_a, sin_b, gq, gk, gv, pm, ws2, bs2, gto)


def _attn_kernel(bound_ref, qT_ref, k_ref, vT_ref, g_ref, o_ref, m_ref, l_ref, acc_ref, s_ref,
                 *, tk, unroll):
    seq = k_ref.shape[1]
    tq = qT_ref.shape[1]
    l_ref[...] = jnp.zeros(l_ref.shape, F32)
    acc_ref[...] = jnp.zeros(acc_ref.shape, F32)
    bound = bound_ref[0]

    def tiles(j, g):
        off = pl.multiple_of(j * tk, tk)
        kg = k_ref[g, pl.ds(off, tk), :]
        vg = vT_ref[g * HEAD_DIM:(g + 1) * HEAD_DIM, pl.ds(off, tk)]
        return kg, vg

    def scores(kg, h, g):
        r0 = h * HEAD_DIM - g * HEAD_DIM
        return _dot(kg, qT_ref[r0:r0 + KV_WIDTH, :])

    def body_bounded(j, carry):
        stages = [(jj, h) for jj in range(unroll) for h in range(N_Q_HEADS)]
        s = s_ref[...]
        for t, (jj, h) in enumerate(stages):
            g = h // Q_PER_KV
            if t + 1 < len(stages):
                jn, hn = stages[t + 1]
                s_next = scores(tiles(j * unroll + jn, hn // Q_PER_KV)[0], hn, hn // Q_PER_KV)
            else:
                j_next = jnp.minimum((j + 1) * unroll, seq // tk - 1)
                s_ref[...] = scores(tiles(j_next, 0)[0], 0, 0)
            p = jnp.exp2(s - bound)
            l_ref[h] += jnp.sum(p.reshape(tk // 8, 8, tq), axis=0)
            rows = slice(h * HEAD_DIM, (h + 1) * HEAD_DIM)
            acc_ref[rows, :] += _dot(tiles(j * unroll + jj, g)[1], p.astype(BF16))
            if t + 1 < len(stages):
                s = s_next
        return carry

    def body_online(j, carry):
        for g in range(N_KV_HEADS):
            kg, vg = tiles(j, g)
            for hh in range(Q_PER_KV):
                h = g * Q_PER_KV + hh
                s = scores(kg, h, g)
                m_prev = m_ref[h:h + 1, :]
                m_new = jnp.maximum(m_prev, jnp.max(s, axis=0, keepdims=True))
                alpha = jnp.exp2(m_prev - m_new)
                p = jnp.exp2(s - m_new)
                l_ref[h] = alpha * l_ref[h] + jnp.sum(p.reshape(tk // 8, 8, tq), axis=0)
                m_ref[h:h + 1, :] = m_new
                rows = slice(h * HEAD_DIM, (h + 1) * HEAD_DIM)
                acc_ref[rows, :] = alpha * acc_ref[rows, :] + _dot(vg, p.astype(BF16))
        return carry

    bounded = bound <= MAX_STATIC_SHIFT

    @pl.when(bounded)
    def _():
        s_ref[...] = scores(tiles(0, 0)[0], 0, 0)
        lax.fori_loop(0, seq // (tk * unroll), body_bounded, 0)

    @pl.when(jnp.logical_not(bounded))
    def _():
        m_ref[...] = jnp.full(m_ref.shape, -1e30, F32)
        lax.fori_loop(0, seq // tk, body_online, 0)

    for h in range(N_Q_HEADS):
        rows = slice(h * HEAD_DIM, (h + 1) * HEAD_DIM)
        inv = 1.0 / jnp.sum(l_ref[h], axis=0, keepdims=True)
        acc_ref[rows, :] = acc_ref[rows, :] * inv
    o = acc_ref[...].T
    o = o * lax.rsqrt(jnp.mean(o * o, axis=-1, keepdims=True) + EPS) * g_ref[...]
    o_ref[...] = o.astype(BF16)


def _attention(bound, qT, k2, vT, g_attn, *, tq, tk, unroll):
    batch, _, seq = qT.shape
    return pl.pallas_call(
        functools.partial(_attn_kernel, tk=tk, unroll=unroll),
        out_shape=jax.ShapeDtypeStruct((batch, seq, ATTN_WIDTH), BF16),
        grid=(batch, seq // tq),
        in_specs=[pl.BlockSpec(memory_space=pltpu.SMEM),
                  pl.BlockSpec((None, ATTN_WIDTH, tq), lambda b, i: (b, 0, i)),
                  pl.BlockSpec((None, N_KV_HEADS, seq, KV_WIDTH), lambda b, i: (b, 0, 0, 0)),
                  pl.BlockSpec((None, KV_WIDTH, seq), lambda b, i: (b, 0, 0)),
                  pl.BlockSpec((1, ATTN_WIDTH), lambda b, i: (0, 0))],
        out_specs=pl.BlockSpec((None, tq, ATTN_WIDTH), lambda b, i: (b, i, 0)),
        scratch_shapes=[pltpu.VMEM((N_Q_HEADS, tq), F32),
                        pltpu.VMEM((N_Q_HEADS, 8, tq), F32),
                        pltpu.VMEM((ATTN_WIDTH, tq), F32),
                        pltpu.VMEM((tk, tq), F32)],
        compiler_params=pltpu.CompilerParams(
            dimension_semantics=("arbitrary", "arbitrary"), vmem_limit_bytes=VMEM_LIMIT_BYTES),
        name="attention",
    )(bound, qT, k2, vT, g_attn)


def _outproj_kernel(a_ref, t_ref, x_ref, wa_ref, wt_ref, gt_ref, g2_ref, sc_ref, sh_ref,
                    x1_ref, h2_ref):
    y = _dot(a_ref[...], wa_ref[...]) + _dot(t_ref[...], wt_ref[...])
    x1 = x_ref[...] + gt_ref[...] * y
    x1_ref[...] = x1
    h = x1 * lax.rsqrt(jnp.mean(x1 * x1, axis=-1, keepdims=True) + EPS) * g2_ref[...]
    h2_ref[...] = (h * (1.0 + sc_ref[...]) + sh_ref[...]).astype(BF16)


def _outproj(attn, tok, x2, w_oa, w_ot, gt1, g2, sc2, sh2, *, seq, tm):
    n, d = x2.shape
    tpb = seq // tm
    const = lambda i: (0, 0)
    per_batch = lambda i: (i // tpb, 0, 0)
    row = lambda i: (i, 0)
    return pl.pallas_call(
        _outproj_kernel,
        out_shape=(jax.ShapeDtypeStruct((n, d), F32), jax.ShapeDtypeStruct((n, d), BF16)),
        grid=(n // tm,),
        in_specs=[pl.BlockSpec((tm, ATTN_WIDTH), row),
                  pl.BlockSpec((tm, TOK_WIDTH), row),
                  pl.BlockSpec((tm, d), row),
                  pl.BlockSpec(w_oa.shape, const, pipeline_mode=pl.Buffered(1)),
                  pl.BlockSpec(w_ot.shape, const, pipeline_mode=pl.Buffered(1)),
                  pl.BlockSpec((None, 1, d), per_batch),
                  pl.BlockSpec((1, d), const),
                  pl.BlockSpec((None, 1, d), per_batch),
                  pl.BlockSpec((None, 1, d), per_batch)],
        out_specs=(pl.BlockSpec((tm, d), row), pl.BlockSpec((tm, d), row)),
        compiler_params=pltpu.CompilerParams(
            dimension_semantics=("arbitrary",), vmem_limit_bytes=VMEM_LIMIT_BYTES),
        name="outproj",
    )(attn, tok, x2, w_oa, w_ot, gt1, g2, sc2, sh2)


def _ffn_kernel(hp_ref, hm_ref, hn_ref, x1_ref, wa_ref, wb_ref, wca_ref, wcb_ref,
                bca_ref, bcb_ref, wd_ref, gt_ref, gf_ref, o_ref, hbuf_ref, acc_ref,
                *, tpb, tn):
    tm = hm_ref.shape[0]
    halo = hp_ref.shape[0]
    d_ff = wa_ref.shape[1]
    i = pl.program_id(0)
    first = (i % tpb) == 0
    last = (i % tpb) == tpb - 1
    hbuf_ref[0:halo, :] = jnp.where(first, jnp.zeros_like(hp_ref[...]), hp_ref[...])
    hbuf_ref[halo:halo + tm, :] = hm_ref[...]
    hbuf_ref[halo + tm:, :] = jnp.where(last, jnp.zeros_like(hn_ref[...]), hn_ref[...])
    acc_ref[...] = jnp.zeros(acc_ref.shape, F32)
    rows = tm + 2 * halo

    def conv(z, wc_ref, bc_ref, cols):
        z_prev = pltpu.roll(z, 1, 0)
        z_next = pltpu.roll(z, rows - 1, 0)
        c = (z_prev * wc_ref[0:1, cols] + z * wc_ref[1:2, cols]
             + z_next * wc_ref[2:3, cols] + bc_ref[:, cols])
        return c[halo:halo + tm, :]

    def body(j, carry):
        cols = pl.ds(pl.multiple_of(j * tn, tn), tn)
        hb = hbuf_ref[...]
        a = conv(_dot(hb, wa_ref[:, cols]), wca_ref, bca_ref, cols)
        b = conv(_dot(hb, wb_ref[:, cols]), wcb_ref, bcb_ref, cols)
        gated = (a / (1.0 + jnp.exp(-a)) * b).astype(BF16)
        acc_ref[...] += _dot(gated, wd_ref[cols, :])
        return carry

    lax.fori_loop(0, d_ff // tn, body, 0)

    x2 = x1_ref[...] + gt_ref[...] * acc_ref[...]
    o_ref[...] = x2 * lax.rsqrt(jnp.mean(x2 * x2, axis=-1, keepdims=True) + EPS) * gf_ref[...]


def _ffn(h2, x1, w_ua, w_ub, wc_a, wc_b, bc_a, bc_b, w_down, gt2, g_final, *, seq, tm, tn):
    n, d = x1.shape
    d_ff = w_down.shape[0]
    halo = BF16_SUBLANES
    tpb = seq // tm
    hb = tm // halo
    last_blk = n // halo - 1
    const = lambda i: (0, 0)
    row = lambda i: (i, 0)
    return pl.pallas_call(
        functools.partial(_ffn_kernel, tpb=tpb, tn=tn),
        out_shape=jax.ShapeDtypeStruct((n, d), F32),
        grid=(n // tm,),
        in_specs=[pl.BlockSpec((halo, d), lambda i: (jnp.maximum(i * hb - 1, 0), 0)),
                  pl.BlockSpec((tm, d), row),
                  pl.BlockSpec((halo, d), lambda i: (jnp.minimum((i + 1) * hb, last_blk), 0)),
                  pl.BlockSpec((tm, d), row),
                  pl.BlockSpec((d, d_ff), const, pipeline_mode=pl.Buffered(1)),
                  pl.BlockSpec((d, d_ff), const, pipeline_mode=pl.Buffered(1)),
                  pl.BlockSpec((3, d_ff), const),
                  pl.BlockSpec((3, d_ff), const),
                  pl.BlockSpec((1, d_ff), const),
                  pl.BlockSpec((1, d_ff), const),
                  pl.BlockSpec((d_ff, d), const, pipeline_mode=pl.Buffered(1)),
                  pl.BlockSpec((None, 1, d), lambda i: (i // tpb, 0, 0)),
                  pl.BlockSpec((1, d), const)],
        out_specs=pl.BlockSpec((tm, d), row),
        scratch_shapes=[pltpu.VMEM((tm + 2 * halo, d), BF16),
                        pltpu.VMEM((tm, d), F32)],
        compiler_params=pltpu.CompilerParams(
            dimension_semantics=("arbitrary",), vmem_limit_bytes=VMEM_LIMIT_BYTES),
        name="ffn",
    )(h2, h2, h2, x1, w_ua, w_ub, wc_a, wc_b, bc_a, bc_b, w_down, gt2, g_final)


def _rope_tables(seq):
    rows = seq // GRID_W
    row_id = jnp.broadcast_to(jnp.arange(rows)[:, None], (rows, GRID_W)).reshape(-1).astype(F32)
    col_id = jnp.broadcast_to(jnp.arange(GRID_W)[None, :], (rows, GRID_W)).reshape(-1).astype(F32)
    axis_dim = HEAD_DIM // 2
    inv_freq = jnp.power(jnp.float32(ROPE_THETA),
                         -jnp.arange(0, axis_dim, 2, dtype=F32) / axis_dim)
    ang_r = row_id[:, None] * inv_freq[None, :]
    ang_c = col_id[:, None] * inv_freq[None, :]
    ang = jnp.concatenate([ang_r, ang_r, ang_c, ang_c], axis=-1)
    cos, sin = jnp.cos(ang), jnp.sin(ang)
    first = (np.arange(HEAD_DIM) % (HEAD_DIM // 2)) < HEAD_DIM // 4
    sin_a = jnp.where(first[None, :], -sin, 0.0)
    sin_b = jnp.where(first[None, :], 0.0, sin)
    rep = LANES // HEAD_DIM
    return jnp.tile(cos, (1, rep)), jnp.tile(sin_a, (1, rep)), jnp.tile(sin_b, (1, rep))


def _block_mean_matrix(width):
    idx = np.arange(width) // HEAD_DIM
    return jnp.asarray((idx[:, None] == idx[None, :]).astype(np.float32) / HEAD_DIM, dtype=BF16)


def kernel(x, c, w_ada, b_ada, g_norm1, w_in, g_q, g_k, g_tok_v, w_s, b_s, g_attn_out,
           g_tok_out, w_out, g_norm2, w_up, w_conv, b_conv, w_down, g_final):
    batch, seq, d = x.shape
    depth = w_ada.shape[0]
    assert depth == 1, "the final RMSNorm is fused into the (single) layer's FFN kernel"
    d_ff = w_down.shape[1]
    n = batch * seq
    tm = 512
    cos, sin_a, sin_b = _rope_tables(seq)
    pm = _block_mean_matrix(256)

    c8 = jnp.zeros((8, d), F32).at[:batch].set(c)
    xf = x.reshape(n, d)
    for l in range(depth):
        mod = _adaln(c8, w_ada[l], b_ada[l][None, :])[:batch]
        sh1, sc1, gt1, sh2, sc2, gt2 = [m[:, None, :] for m in jnp.split(mod, 6, axis=-1)]

        tile_h = lambda g, w: jnp.tile(g, w // HEAD_DIM)[None, :]
        ws2 = jnp.concatenate([w_s[l][0::2], w_s[l][1::2]], axis=-1).astype(BF16)
        bs2 = jnp.repeat(b_s[l].reshape(N_TOK_HEADS // 2, 2, CHUNK), HEAD_DIM, axis=1)
        bs2 = jnp.swapaxes(bs2, 1, 2)

        qT, k2, vT, tok = _inproj(
            xf, sh1, sc1, g_norm1[l][None, :], w_in[l].astype(BF16), cos, sin_a, sin_b,
            tile_h(g_q[l], ATTN_WIDTH), tile_h(g_k[l], KV_WIDTH), tile_h(g_tok_v[l], TOK_WIDTH),
            pm, ws2, bs2, g_tok_out[l][None, :], batch=batch, seq=seq, tm=tm)

        bound = (Q_SCALE * HEAD_DIM * BF16_ROUNDING_MARGIN
                 * jnp.max(jnp.abs(g_q[l])) * jnp.max(jnp.abs(g_k[l]))).reshape(1).astype(F32)
        attn = _attention(bound, qT, k2, vT, g_attn_out[l][None, :], tq=512, tk=256, unroll=2)

        w_o = w_out[l].astype(BF16)
        x1, h2 = _outproj(attn.reshape(n, ATTN_WIDTH), tok, xf, w_o[:ATTN_WIDTH], w_o[ATTN_WIDTH:],
                          gt1, g_norm2[l][None, :], sc2, sh2, seq=seq, tm=tm)

        w_u = w_up[l].astype(BF16)
        xf = _ffn(h2, x1, w_u[:, :d_ff], w_u[:, d_ff:], w_conv[l][:, :d_ff], w_conv[l][:, d_ff:],
                  b_conv[l][None, :d_ff], b_conv[l][None, d_ff:], w_down[l].astype(BF16),
                  gt2, g_final[None, :], seq=seq, tm=tm, tn=256)
    return xf.reshape(batch, seq, d)
```

```python
import functools
import math

import jax
import jax.numpy as jnp
import numpy as np
from jax import lax
from jax.experimental import pallas as pl
from jax.experimental.pallas import tpu as pltpu

HEAD_DIM = 64
N_Q_HEADS = 8
N_KV_HEADS = 2
Q_PER_KV = N_Q_HEADS // N_KV_HEADS
ATTN_WIDTH = N_Q_HEADS * HEAD_DIM
KV_WIDTH = N_KV_HEADS * HEAD_DIM
TOK_WIDTH = 512
N_TOK_HEADS = TOK_WIDTH // HEAD_DIM
CHUNK = 128
GRID_W = 64
ROPE_THETA = 10000.0
EPS = 1e-6
LANES = 128
BF16_SUBLANES = 16
VMEM_LIMIT_BYTES = 56 * 1024 * 1024

Q_SCALE = (HEAD_DIM ** -0.5) * math.log2(math.e)
MAX_STATIC_SHIFT = 60.0
BF16_ROUNDING_MARGIN = (1.0 + 2.0 ** -7) ** 2

F32 = jnp.float32
BF16 = jnp.bfloat16


def _dot(a, b):
    return jnp.dot(a, b, preferred_element_type=F32)


def _head_mean_square(y, pm):
    yy = y * y
    hi = yy.astype(BF16)
    lo = (yy - hi.astype(F32)).astype(BF16)
    return _dot(hi, pm) + _dot(lo, pm)


def _rope(y, cos, sin_a, sin_b):
    w = y.shape[-1]
    outs = []
    for c in range(w // LANES):
        yc = y[:, c * LANES:(c + 1) * LANES]
        up = pltpu.roll(yc, LANES - HEAD_DIM // 4, 1)
        dn = pltpu.roll(yc, HEAD_DIM // 4, 1)
        outs.append(yc * cos + up * sin_a + dn * sin_b)
    return outs[0] if len(outs) == 1 else jnp.concatenate(outs, axis=-1)


def _adaln_kernel(c_ref, w_ref, b_ref, o_ref):
    c = c_ref[...]
    a = c / (1.0 + jnp.exp(-c))
    o_ref[...] = jnp.dot(a, w_ref[...], preferred_element_type=F32,
                         precision=lax.Precision.HIGHEST) + b_ref[...]


def _adaln(c8, w_ada, b_ada):
    d, n = w_ada.shape
    tn = 1536
    return pl.pallas_call(
        _adaln_kernel,
        out_shape=jax.ShapeDtypeStruct((c8.shape[0], n), F32),
        grid=(n // tn,),
        in_specs=[pl.BlockSpec((c8.shape[0], d), lambda j: (0, 0)),
                  pl.BlockSpec((d, tn), lambda j: (0, j)),
                  pl.BlockSpec((1, tn), lambda j: (0, j))],
        out_specs=pl.BlockSpec((c8.shape[0], tn), lambda j: (0, j)),
        compiler_params=pltpu.CompilerParams(
            dimension_semantics=("arbitrary",), vmem_limit_bytes=VMEM_LIMIT_BYTES),
        name="adaln",
    )(c8, w_ada, b_ada)


def _inproj_kernel(x_ref, sh_ref, sc_ref, g1_ref, w_ref, cos_ref, sa_ref, sb_ref,
                   gq_ref, gk_ref, gv_ref, pm_ref, ws_ref, bs_ref, gto_ref,
                   qT_ref, k_ref, vT_ref, tok_ref):
    tm = x_ref.shape[0]
    x = x_ref[...]
    h = x * lax.rsqrt(jnp.mean(x * x, axis=-1, keepdims=True) + EPS) * g1_ref[...]
    h = h * (1.0 + sc_ref[...]) + sh_ref[...]
    proj = _dot(h.astype(BF16), w_ref[...])

    cos, sa, sb = cos_ref[...], sa_ref[...], sb_ref[...]
    pm = pm_ref[...]

    q = proj[:, :ATTN_WIDTH]
    ms = jnp.concatenate([_head_mean_square(q[:, c:c + 256], pm)
                          for c in range(0, ATTN_WIDTH, 256)], axis=-1)
    q = q * lax.rsqrt(ms + EPS) * gq_ref[...]
    q = _rope(q, cos, sa, sb) * Q_SCALE
    qT_ref[...] = q.T.astype(BF16)

    k = proj[:, ATTN_WIDTH:ATTN_WIDTH + KV_WIDTH]
    ms = _head_mean_square(k, pm[:KV_WIDTH, :KV_WIDTH])
    k = k * lax.rsqrt(ms + EPS) * gk_ref[...]
    k = _rope(k, cos, sa, sb)
    lane = lax.broadcasted_iota(jnp.int32, k.shape, 1)
    k_ref[0] = jnp.where(lane < HEAD_DIM, k, 0.0).astype(BF16)
    k_ref[1] = jnp.where(lane >= HEAD_DIM, k, 0.0).astype(BF16)

    v = proj[:, ATTN_WIDTH + KV_WIDTH:ATTN_WIDTH + 2 * KV_WIDTH]
    vT_ref[...] = v.T.astype(BF16)

    z0 = ATTN_WIDTH + 2 * KV_WIDTH
    z = jax.nn.gelu(proj[:, z0:], approximate=True)
    u = z[:, :TOK_WIDTH]
    vv = z[:, TOK_WIDTH:]
    ms = jnp.concatenate([_head_mean_square(vv[:, c:c + 256], pm)
                          for c in range(0, TOK_WIDTH, 256)], axis=-1)
    vn = (vv * lax.rsqrt(ms + EPS) * gv_ref[...]).astype(BF16)

    lane_b = lax.broadcasted_iota(jnp.int32, (CHUNK, LANES), 1)
    zero = jnp.zeros((CHUNK, LANES), BF16)
    rows = []
    for c in range(tm // CHUNK):
        cols = []
        for jp in range(N_TOK_HEADS // 2):
            vt = vn[c * CHUNK:(c + 1) * CHUNK, jp * LANES:(jp + 1) * LANES]
            rhs = jnp.concatenate([jnp.where(lane_b < HEAD_DIM, vt, zero),
                                   jnp.where(lane_b >= HEAD_DIM, vt, zero)], axis=0)
            cols.append(_dot(ws_ref[jp], rhs) + bs_ref[jp])
        rows.append(jnp.concatenate(cols, axis=-1))
    tok = u * jnp.concatenate(rows, axis=0)
    tok = tok * lax.rsqrt(jnp.mean(tok * tok, axis=-1, keepdims=True) + EPS) * gto_ref[...]
    tok_ref[...] = tok.astype(BF16)


def _inproj(x2, sh1, sc1, g1, w_in, cos, sin_a, sin_b, gq, gk, gv, pm, ws2, bs2, gto,
            *, batch, seq, tm):
    n, d = x2.shape
    tpb = seq // tm
    n_in = w_in.shape[1]
    const = lambda i: (0, 0)
    const3 = lambda i: (0, 0, 0)
    per_batch = lambda i: (i // tpb, 0, 0)
    return pl.pallas_call(
        _inproj_kernel,
        out_shape=(jax.ShapeDtypeStruct((batch, ATTN_WIDTH, seq), BF16),
                   jax.ShapeDtypeStruct((batch, N_KV_HEADS, seq, KV_WIDTH), BF16),
                   jax.ShapeDtypeStruct((batch, KV_WIDTH, seq), BF16),
                   jax.ShapeDtypeStruct((n, TOK_WIDTH), BF16)),
        grid=(n // tm,),
        in_specs=[pl.BlockSpec((tm, d), lambda i: (i, 0)),
                  pl.BlockSpec((None, 1, d), per_batch),
                  pl.BlockSpec((None, 1, d), per_batch),
                  pl.BlockSpec((1, d), const),
                  pl.BlockSpec((d, n_in), const, pipeline_mode=pl.Buffered(1)),
                  pl.BlockSpec((tm, LANES), lambda i: (i % tpb, 0)),
                  pl.BlockSpec((tm, LANES), lambda i: (i % tpb, 0)),
                  pl.BlockSpec((tm, LANES), lambda i: (i % tpb, 0)),
                  pl.BlockSpec((1, ATTN_WIDTH), const),
                  pl.BlockSpec((1, KV_WIDTH), const),
                  pl.BlockSpec((1, TOK_WIDTH), const),
                  pl.BlockSpec((256, 256), const),
                  pl.BlockSpec(ws2.shape, const3),
                  pl.BlockSpec(bs2.shape, const3),
                  pl.BlockSpec((1, TOK_WIDTH), const)],
        out_specs=(pl.BlockSpec((None, ATTN_WIDTH, tm), lambda i: (i // tpb, 0, i % tpb)),
                   pl.BlockSpec((None, N_KV_HEADS, tm, KV_WIDTH),
                                lambda i: (i // tpb, 0, i % tpb, 0)),
                   pl.BlockSpec((None, KV_WIDTH, tm), lambda i: (i // tpb, 0, i % tpb)),
                   pl.BlockSpec((tm, TOK_WIDTH), lambda i: (i, 0))),
        compiler_params=pltpu.CompilerParams(
            dimension_semantics=("arbitrary",), vmem_limit_bytes=VMEM_LIMIT_BYTES),
        name="inproj",
    )(x2, sh1, sc1, g1, w_in, cos, sin_a, sin_b, gq, gk, gv, pm, ws2, bs2, gto)


def _attn_kernel(bound_ref, qT_ref, k_ref, vT_ref, g_ref, o_ref, m_ref, l_ref, acc_ref, s_ref,
                 *, tk, unroll):
    seq = k_ref.shape[1]
    tq = qT_ref.shape[1]
    l_ref[...] = jnp.zeros(l_ref.shape, F32)
    acc_ref[...] = jnp.zeros(acc_ref.shape, F32)
    bound = bound_ref[0]

    def tiles(j, g):
        off = pl.multiple_of(j * tk, tk)
        kg = k_ref[g, pl.ds(off, tk), :]
        vg = vT_ref[g * HEAD_DIM:(g + 1) * HEAD_DIM, pl.ds(off, tk)]
        return kg, vg

    def scores(kg, h, g):
        r0 = h * HEAD_DIM - g * HEAD_DIM
        return _dot(kg, qT_ref[r0:r0 + KV_WIDTH, :])

    def body_bounded(j, carry):
        stages = [(jj, h) for jj in range(unroll) for h in range(N_Q_HEADS)]
        s = s_ref[...]
        for t, (jj, h) in enumerate(stages):
            g = h // Q_PER_KV
            if t + 1 < len(stages):
                jn, hn = stages[t + 1]
                s_next = scores(tiles(j * unroll + jn, hn // Q_PER_KV)[0], hn, hn // Q_PER_KV)
            else:
                j_next = jnp.minimum((j + 1) * unroll, seq // tk - 1)
                s_ref[...] = scores(tiles(j_next, 0)[0], 0, 0)
            p = jnp.exp2(s - bound)
            l_ref[h] += jnp.sum(p.reshape(tk // 8, 8, tq), axis=0)
            rows = slice(h * HEAD_DIM, (h + 1) * HEAD_DIM)
            acc_ref[rows, :] += _dot(tiles(j * unroll + jj, g)[1], p.astype(BF16))
            if t + 1 < len(stages):
                s = s_next
        return carry

    def body_online(j, carry):
        for g in range(N_KV_HEADS):
            kg, vg = tiles(j, g)
            for hh in range(Q_PER_KV):
                h = g * Q_PER_KV + hh
                s = scores(kg, h, g)
                m_prev = m_ref[h:h + 1, :]
                m_new = jnp.maximum(m_prev, jnp.max(s, axis=0, keepdims=True))
                alpha = jnp.exp2(m_prev - m_new)
                p = jnp.exp2(s - m_new)
                l_ref[h] = alpha * l_ref[h] + jnp.sum(p.reshape(tk // 8, 8, tq), axis=0)
                m_ref[h:h + 1, :] = m_new
                rows = slice(h * HEAD_DIM, (h + 1) * HEAD_DIM)
                acc_ref[rows, :] = alpha * acc_ref[rows, :] + _dot(vg, p.astype(BF16))
        return carry

    bounded = bound <= MAX_STATIC_SHIFT

    @pl.when(bounded)
    def _():
        s_ref[...] = scores(tiles(0, 0)[0], 0, 0)
        lax.fori_loop(0, seq // (tk * unroll), body_bounded, 0)

    @pl.when(jnp.logical_not(bounded))
    def _():
        m_ref[...] = jnp.full(m_ref.shape, -1e30, F32)
        lax.fori_loop(0, seq // tk, body_online, 0)

    for h in range(N_Q_HEADS):
        rows = slice(h * HEAD_DIM, (h + 1) * HEAD_DIM)
        inv = 1.0 / jnp.sum(l_ref[h], axis=0, keepdims=True)
        acc_ref[rows, :] = acc_ref[rows, :] * inv
    o = acc_ref[...].T
    o = o * lax.rsqrt(jnp.mean(o * o, axis=-1, keepdims=True) + EPS) * g_ref[...]
    o_ref[...] = o.astype(BF16)


def _attention(bound, qT, k2, vT, g_attn, *, tq, tk, unroll):
    batch, _, seq = qT.shape
    return pl.pallas_call(
        functools.partial(_attn_kernel, tk=tk, unroll=unroll),
        out_shape=jax.ShapeDtypeStruct((batch, seq, ATTN_WIDTH), BF16),
        grid=(batch, seq // tq),
        in_specs=[pl.BlockSpec(memory_space=pltpu.SMEM),
                  pl.BlockSpec((None, ATTN_WIDTH, tq), lambda b, i: (b, 0, i)),
                  pl.BlockSpec((None, N_KV_HEADS, seq, KV_WIDTH), lambda b, i: (b, 0, 0, 0)),
                  pl.BlockSpec((None, KV_WIDTH, seq), lambda b, i: (b, 0, 0)),
                  pl.BlockSpec((1, ATTN_WIDTH), lambda b, i: (0, 0))],
        out_specs=pl.BlockSpec((None, tq, ATTN_WIDTH), lambda b, i: (b, i, 0)),
        scratch_shapes=[pltpu.VMEM((N_Q_HEADS, tq), F32),
                        pltpu.VMEM((N_Q_HEADS, 8, tq), F32),
                        pltpu.VMEM((ATTN_WIDTH, tq), F32),
                        pltpu.VMEM((tk, tq), F32)],
        compiler_params=pltpu.CompilerParams(
            dimension_semantics=("arbitrary", "arbitrary"), vmem_limit_bytes=VMEM_LIMIT_BYTES),
        name="attention",
    )(bound, qT, k2, vT, g_attn)


def _outproj_kernel(a_ref, t_ref, x_ref, wa_ref, wt_ref, gt_ref, g2_ref, sc_ref, sh_ref,
                    x1_ref, h2_ref):
    y = _dot(a_ref[...], wa_ref[...]) + _dot(t_ref[...], wt_ref[...])
    x1 = x_ref[...] + gt_ref[...] * y
    x1_ref[...] = x1
    h = x1 * lax.rsqrt(jnp.mean(x1 * x1, axis=-1, keepdims=True) + EPS) * g2_ref[...]
    h2_ref[...] = (h * (1.0 + sc_ref[...]) + sh_ref[...]).astype(BF16)


def _outproj(attn, tok, x2, w_oa, w_ot, gt1, g2, sc2, sh2, *, seq, tm):
    n, d = x2.shape
    tpb = seq // tm
    const = lambda i: (0, 0)
    per_batch = lambda i: (i // tpb, 0, 0)
    row = lambda i: (i, 0)
    return pl.pallas_call(
        _outproj_kernel,
        out_shape=(jax.ShapeDtypeStruct((n, d), F32), jax.ShapeDtypeStruct((n, d), BF16)),
        grid=(n // tm,),
        in_specs=[pl.BlockSpec((tm, ATTN_WIDTH), row),
                  pl.BlockSpec((tm, TOK_WIDTH), row),
                  pl.BlockSpec((tm, d), row),
                  pl.BlockSpec(w_oa.shape, const, pipeline_mode=pl.Buffered(1)),
                  pl.BlockSpec(w_ot.shape, const, pipeline_mode=pl.Buffered(1)),
                  pl.BlockSpec((None, 1, d), per_batch),
                  pl.BlockSpec((1, d), const),
                  pl.BlockSpec((None, 1, d), per_batch),
                  pl.BlockSpec((None, 1, d), per_batch)],
        out_specs=(pl.BlockSpec((tm, d), row), pl.BlockSpec((tm, d), row)),
        compiler_params=pltpu.CompilerParams(
            dimension_semantics=("arbitrary",), vmem_limit_bytes=VMEM_LIMIT_BYTES),
        name="outproj",
    )(attn, tok, x2, w_oa, w_ot, gt1, g2, sc2, sh2)


def _ffn_kernel(hp_ref, hm_ref, hn_ref, x1_ref, wa_ref, wb_ref, wca_ref, wcb_ref,
                bca_ref, bcb_ref, wd_ref, gt_ref, gf_ref, o_ref, hbuf_ref, acc_ref, z_ref,
                *, tpb, tn):
    tm = hm_ref.shape[0]
    halo = hp_ref.shape[0]
    d_ff = wa_ref.shape[1]
    i = pl.program_id(0)
    first = (i % tpb) == 0
    last = (i % tpb) == tpb - 1
    hbuf_ref[0:halo, :] = jnp.where(first, jnp.zeros_like(hp_ref[...]), hp_ref[...])
    hbuf_ref[halo:halo + tm, :] = hm_ref[...]
    hbuf_ref[halo + tm:, :] = jnp.where(last, jnp.zeros_like(hn_ref[...]), hn_ref[...])
    acc_ref[...] = jnp.zeros(acc_ref.shape, F32)
    rows = tm + 2 * halo

    def conv(z, wc_ref, bc_ref, cols):
        z_prev = pltpu.roll(z, 1, 0)
        z_next = pltpu.roll(z, rows - 1, 0)
        c = (z_prev * wc_ref[0:1, cols] + z * wc_ref[1:2, cols]
             + z_next * wc_ref[2:3, cols] + bc_ref[:, cols])
        return c[halo:halo + tm, :]

    def up(j, slot):
        cols = pl.ds(pl.multiple_of(j * tn, tn), tn)
        hb = hbuf_ref[...]
        z_ref[slot, 0] = _dot(hb, wa_ref[:, cols])
        z_ref[slot, 1] = _dot(hb, wb_ref[:, cols])

    def down(j, slot):
        cols = pl.ds(pl.multiple_of(j * tn, tn), tn)
        a = conv(z_ref[slot, 0], wca_ref, bca_ref, cols)
        b = conv(z_ref[slot, 1], wcb_ref, bcb_ref, cols)
        gated = (a / (1.0 + jnp.exp(-a)) * b).astype(BF16)
        acc_ref[...] += _dot(gated, wd_ref[cols, :])

    nt = d_ff // tn
    up(0, 0)

    def body(i2, carry):
        j = 2 * i2
        up(j + 1, 1)
        down(j, 0)
        up(j + 2, 0)
        down(j + 1, 1)
        return carry

    pairs = (nt - 1) // 2
    lax.fori_loop(0, pairs, body, 0)
    for t in range(2 * pairs, nt):
        slot = (t - 2 * pairs) % 2
        if t + 1 < nt:
            up(t + 1, 1 - slot)
        down(t, slot)

    x2 = x1_ref[...] + gt_ref[...] * acc_ref[...]
    o_ref[...] = x2 * lax.rsqrt(jnp.mean(x2 * x2, axis=-1, keepdims=True) + EPS) * gf_ref[...]


def _ffn(h2, x1, w_ua, w_ub, wc_a, wc_b, bc_a, bc_b, w_down, gt2, g_final, *, seq, tm, tn):
    n, d = x1.shape
    d_ff = w_down.shape[0]
    halo = BF16_SUBLANES
    tpb = seq // tm
    hb = tm // halo
    last_blk = n // halo - 1
    const = lambda i: (0, 0)
    row = lambda i: (i, 0)
    return pl.pallas_call(
        functools.partial(_ffn_kernel, tpb=tpb, tn=tn),
        out_shape=jax.ShapeDtypeStruct((n, d), F32),
        grid=(n // tm,),
        in_specs=[pl.BlockSpec((halo, d), lambda i: (jnp.maximum(i * hb - 1, 0), 0)),
                  pl.BlockSpec((tm, d), row),
                  pl.BlockSpec((halo, d), lambda i: (jnp.minimum((i + 1) * hb, last_blk), 0)),
                  pl.BlockSpec((tm, d), row),
                  pl.BlockSpec((d, d_ff), const, pipeline_mode=pl.Buffered(1)),
                  pl.BlockSpec((d, d_ff), const, pipeline_mode=pl.Buffered(1)),
                  pl.BlockSpec((3, d_ff), const),
                  pl.BlockSpec((3, d_ff), const),
                  pl.BlockSpec((1, d_ff), const),
                  pl.BlockSpec((1, d_ff), const),
                  pl.BlockSpec((d_ff, d), const, pipeline_mode=pl.Buffered(1)),
                  pl.BlockSpec((None, 1, d), lambda i: (i // tpb, 0, 0)),
                  pl.BlockSpec((1, d), const)],
        out_specs=pl.BlockSpec((tm, d), row),
        scratch_shapes=[pltpu.VMEM((tm + 2 * halo, d), BF16),
                        pltpu.VMEM((tm, d), F32),
                        pltpu.VMEM((2, 2, tm + 2 * halo, tn), F32)],
        compiler_params=pltpu.CompilerParams(
            dimension_semantics=("arbitrary",), vmem_limit_bytes=VMEM_LIMIT_BYTES),
        name="ffn",
    )(h2, h2, h2, x1, w_ua, w_ub, wc_a, wc_b, bc_a, bc_b, w_down, gt2, g_final)


def _rope_tables(seq):
    rows = seq // GRID_W
    row_id = jnp.broadcast_to(jnp.arange(rows)[:, None], (rows, GRID_W)).reshape(-1).astype(F32)
    col_id = jnp.broadcast_to(jnp.arange(GRID_W)[None, :], (rows, GRID_W)).reshape(-1).astype(F32)
    axis_dim = HEAD_DIM // 2
    inv_freq = jnp.power(jnp.float32(ROPE_THETA),
                         -jnp.arange(0, axis_dim, 2, dtype=F32) / axis_dim)
    ang_r = row_id[:, None] * inv_freq[None, :]
    ang_c = col_id[:, None] * inv_freq[None, :]
    ang = jnp.concatenate([ang_r, ang_r, ang_c, ang_c], axis=-1)
    cos, sin = jnp.cos(ang), jnp.sin(ang)
    first = (np.arange(HEAD_DIM) % (HEAD_DIM // 2)) < HEAD_DIM // 4
    sin_a = jnp.where(first[None, :], -sin, 0.0)
    sin_b = jnp.where(first[None, :], 0.0, sin)
    rep = LANES // HEAD_DIM
    return jnp.tile(cos, (1, rep)), jnp.tile(sin_a, (1, rep)), jnp.tile(sin_b, (1, rep))


def _block_mean_matrix(width):
    idx = np.arange(width) // HEAD_DIM
    return jnp.asarray((idx[:, None] == idx[None, :]).astype(np.float32) / HEAD_DIM, dtype=BF16)


def kernel(x, c, w_ada, b_ada, g_norm1, w_in, g_q, g_k, g_tok_v, w_s, b_s, g_attn_out,
           g_tok_out, w_out, g_norm2, w_up, w_conv, b_conv, w_down, g_final):
    batch, seq, d = x.shape
    depth = w_ada.shape[0]
    assert depth == 1, "the final RMSNorm is fused into the (single) layer's FFN kernel"
    d_ff = w_down.shape[1]
    n = batch * seq
    tm = 512
    cos, sin_a, sin_b = _rope_tables(seq)
    pm = _block_mean_matrix(256)

    c8 = jnp.zeros((8, d), F32).at[:batch].set(c)
    xf = x.reshape(n, d)
    for l in range(depth):
        mod = _adaln(c8, w_ada[l], b_ada[l][None, :])[:batch]
        sh1, sc1, gt1, sh2, sc2, gt2 = [m[:, None, :] for m in jnp.split(mod, 6, axis=-1)]

        tile_h = lambda g, w: jnp.tile(g, w // HEAD_DIM)[None, :]
        ws2 = jnp.concatenate([w_s[l][0::2], w_s[l][1::2]], axis=-1).astype(BF16)
        bs2 = jnp.repeat(b_s[l].reshape(N_TOK_HEADS // 2, 2, CHUNK), HEAD_DIM, axis=1)
        bs2 = jnp.swapaxes(bs2, 1, 2)

        qT, k2, vT, tok = _inproj(
            xf, sh1, sc1, g_norm1[l][None, :], w_in[l].astype(BF16), cos, sin_a, sin_b,
            tile_h(g_q[l], ATTN_WIDTH), tile_h(g_k[l], KV_WIDTH), tile_h(g_tok_v[l], TOK_WIDTH),
            pm, ws2, bs2, g_tok_out[l][None, :], batch=batch, seq=seq, tm=tm)

        bound = (Q_SCALE * HEAD_DIM * BF16_ROUNDING_MARGIN
                 * jnp.max(jnp.abs(g_q[l])) * jnp.max(jnp.abs(g_k[l]))).reshape(1).astype(F32)
        attn = _attention(bound, qT, k2, vT, g_attn_out[l][None, :], tq=512, tk=256, unroll=2)

        w_o = w_out[l].astype(BF16)
        x1, h2 = _outproj(attn.reshape(n, ATTN_WIDTH), tok, xf, w_o[:ATTN_WIDTH], w_o[ATTN_WIDTH:],
                          gt1, g_norm2[l][None, :], sc2, sh2, seq=seq, tm=tm)

        w_u = w_up[l].astype(BF16)
        xf = _ffn(h2, x1, w_u[:, :d_ff], w_u[:, d_ff:], w_conv[l][:, :d_ff], w_conv[l][:, d_ff:],
                  b_conv[l][None, :d_ff], b_conv[l][None, d_ff:], w_down[l].astype(BF16),
                  gt2, g_final[None, :], seq=seq, tm=tm, tn=256)
    return xf.reshape(batch, seq, d)
```

```python
import functools
import math

import jax
import jax.numpy as jnp
import numpy as np
from jax import lax
from jax.experimental import pallas as pl
from jax.experimental.pallas import tpu as pltpu

HEAD_DIM = 64
N_Q_HEADS = 8
N_KV_HEADS = 2
Q_PER_KV = N_Q_HEADS // N_KV_HEADS
ATTN_WIDTH = N_Q_HEADS * HEAD_DIM
KV_WIDTH = N_KV_HEADS * HEAD_DIM
TOK_WIDTH = 512
N_TOK_HEADS = TOK_WIDTH // HEAD_DIM
CHUNK = 128
GRID_W = 64
ROPE_THETA = 10000.0
EPS = 1e-6
LANES = 128
BF16_SUBLANES = 16
VMEM_LIMIT_BYTES = 56 * 1024 * 1024

Q_SCALE = (HEAD_DIM ** -0.5) * math.log2(math.e)
MAX_STATIC_SHIFT = 60.0
BF16_ROUNDING_MARGIN = (1.0 + 2.0 ** -7) ** 2

F32 = jnp.float32
BF16 = jnp.bfloat16


def _dot(a, b):
    return jnp.dot(a, b, preferred_element_type=F32)


def _head_mean_square(y, pm):
    yy = y * y
    hi = yy.astype(BF16)
    lo = (yy - hi.astype(F32)).astype(BF16)
    return _dot(hi, pm) + _dot(lo, pm)


def _rope(y, cos, sin_a, sin_b):
    w = y.shape[-1]
    outs = []
    for c in range(w // LANES):
        yc = y[:, c * LANES:(c + 1) * LANES]
        up = pltpu.roll(yc, LANES - HEAD_DIM // 4, 1)
        dn = pltpu.roll(yc, HEAD_DIM // 4, 1)
        outs.append(yc * cos + up * sin_a + dn * sin_b)
    return outs[0] if len(outs) == 1 else jnp.concatenate(outs, axis=-1)


def _adaln_kernel(c_ref, w_ref, b_ref, o_ref):
    c = c_ref[...]
    a = c / (1.0 + jnp.exp(-c))
    o_ref[...] = jnp.dot(a, w_ref[...], preferred_element_type=F32,
                         precision=lax.Precision.HIGHEST) + b_ref[...]


def _adaln(c8, w_ada, b_ada):
    d, n = w_ada.shape
    tn = 1536
    return pl.pallas_call(
        _adaln_kernel,
        out_shape=jax.ShapeDtypeStruct((c8.shape[0], n), F32),
        grid=(n // tn,),
        in_specs=[pl.BlockSpec((c8.shape[0], d), lambda j: (0, 0)),
                  pl.BlockSpec((d, tn), lambda j: (0, j)),
                  pl.BlockSpec((1, tn), lambda j: (0, j))],
        out_specs=pl.BlockSpec((c8.shape[0], tn), lambda j: (0, j)),
        compiler_params=pltpu.CompilerParams(
            dimension_semantics=("arbitrary",), vmem_limit_bytes=VMEM_LIMIT_BYTES),
        name="adaln",
    )(c8, w_ada, b_ada)


def _inproj_kernel(x_ref, sh_ref, sc_ref, g1_ref, w_ref, cos_ref, sa_ref, sb_ref,
                   gq_ref, gk_ref, gv_ref, pm_ref, ws_ref, bs_ref, gto_ref,
                   qT_ref, k_ref, vT_ref, tok_ref):
    tm = x_ref.shape[0]
    x = x_ref[...]
    h = x * lax.rsqrt(jnp.mean(x * x, axis=-1, keepdims=True) + EPS) * g1_ref[...]
    h = h * (1.0 + sc_ref[...]) + sh_ref[...]
    proj = _dot(h.astype(BF16), w_ref[...])

    cos, sa, sb = cos_ref[...], sa_ref[...], sb_ref[...]
    pm = pm_ref[...]

    q = proj[:, :ATTN_WIDTH]
    ms = jnp.concatenate([_head_mean_square(q[:, c:c + 256], pm)
                          for c in range(0, ATTN_WIDTH, 256)], axis=-1)
    q = q * lax.rsqrt(ms + EPS) * gq_ref[...]
    q = _rope(q, cos, sa, sb) * Q_SCALE
    qT_ref[...] = q.T.astype(BF16)

    k = proj[:, ATTN_WIDTH:ATTN_WIDTH + KV_WIDTH]
    ms = _head_mean_square(k, pm[:KV_WIDTH, :KV_WIDTH])
    k = k * lax.rsqrt(ms + EPS) * gk_ref[...]
    k = _rope(k, cos, sa, sb)
    lane = lax.broadcasted_iota(jnp.int32, k.shape, 1)
    k_ref[0] = jnp.where(lane < HEAD_DIM, k, 0.0).astype(BF16)
    k_ref[1] = jnp.where(lane >= HEAD_DIM, k, 0.0).astype(BF16)

    v = proj[:, ATTN_WIDTH + KV_WIDTH:ATTN_WIDTH + 2 * KV_WIDTH]
    vT_ref[...] = v.T.astype(BF16)

    z0 = ATTN_WIDTH + 2 * KV_WIDTH
    z = jax.nn.gelu(proj[:, z0:], approximate=True)
    u = z[:, :TOK_WIDTH]
    vv = z[:, TOK_WIDTH:]
    ms = jnp.concatenate([_head_mean_square(vv[:, c:c + 256], pm)
                          for c in range(0, TOK_WIDTH, 256)], axis=-1)
    vn = (vv * lax.rsqrt(ms + EPS) * gv_ref[...]).astype(BF16)

    lane_b = lax.broadcasted_iota(jnp.int32, (CHUNK, LANES), 1)
    zero = jnp.zeros((CHUNK, LANES), BF16)
    rows = []
    for c in range(tm // CHUNK):
        cols = []
        for jp in range(N_TOK_HEADS // 2):
            vt = vn[c * CHUNK:(c + 1) * CHUNK, jp * LANES:(jp + 1) * LANES]
            rhs = jnp.concatenate([jnp.where(lane_b < HEAD_DIM, vt, zero),
                                   jnp.where(lane_b >= HEAD_DIM, vt, zero)], axis=0)
            cols.append(_dot(ws_ref[jp], rhs) + bs_ref[jp])
        rows.append(jnp.concatenate(cols, axis=-1))
    tok = u * jnp.concatenate(rows, axis=0)
    tok = tok * lax.rsqrt(jnp.mean(tok * tok, axis=-1, keepdims=True) + EPS) * gto_ref[...]
    tok_ref[...] = tok.astype(BF16)


def _inproj(x2, sh1, sc1, g1, w_in, cos, sin_a, sin_b, gq, gk, gv, pm, ws2, bs2, gto,
            *, batch, seq, tm):
    n, d = x2.shape
    tpb = seq // tm
    n_in = w_in.shape[1]
    const = lambda i: (0, 0)
    const3 = lambda i: (0, 0, 0)
    per_batch = lambda i: (i // tpb, 0, 0)
    return pl.pallas_call(
        _inproj_kernel,
        out_shape=(jax.ShapeDtypeStruct((batch, ATTN_WIDTH, seq), BF16),
                   jax.ShapeDtypeStruct((batch, N_KV_HEADS, seq, KV_WIDTH), BF16),
                   jax.ShapeDtypeStruct((batch, KV_WIDTH, seq), BF16),
                   jax.ShapeDtypeStruct((n, TOK_WIDTH), BF16)),
        grid=(n // tm,),
        in_specs=[pl.BlockSpec((tm, d), lambda i: (i, 0)),
                  pl.BlockSpec((None, 1, d), per_batch),
                  pl.BlockSpec((None, 1, d), per_batch),
                  pl.BlockSpec((1, d), const),
                  pl.BlockSpec((d, n_in), const, pipeline_mode=pl.Buffered(1)),
                  pl.BlockSpec((tm, LANES), lambda i: (i % tpb, 0)),
                  pl.BlockSpec((tm, LANES), lambda i: (i % tpb, 0)),
                  pl.BlockSpec((tm, LANES), lambda i: (i % tpb, 0)),
                  pl.BlockSpec((1, ATTN_WIDTH), const),
                  pl.BlockSpec((1, KV_WIDTH), const),
                  pl.BlockSpec((1, TOK_WIDTH), const),
                  pl.BlockSpec((256, 256), const),
                  pl.BlockSpec(ws2.shape, const3),
                  pl.BlockSpec(bs2.shape, const3),
                  pl.BlockSpec((1, TOK_WIDTH), const)],
        out_specs=(pl.BlockSpec((None, ATTN_WIDTH, tm), lambda i: (i // tpb, 0, i % tpb)),
                   pl.BlockSpec((None, N_KV_HEADS, tm, KV_WIDTH),
                                lambda i: (i // tpb, 0, i % tpb, 0)),
                   pl.BlockSpec((None, KV_WIDTH, tm), lambda i: (i // tpb, 0, i % tpb)),
                   pl.BlockSpec((tm, TOK_WIDTH), lambda i: (i, 0))),
        compiler_params=pltpu.CompilerParams(
            dimension_semantics=("arbitrary",), vmem_limit_bytes=VMEM_LIMIT_BYTES),
        name="inproj",
    )(x2, sh1, sc1, g1, w_in, cos, sin_a, sin_b, gq, gk, gv, pm, ws2, bs2, gto)


def _attn_kernel(bound_ref, qT_ref, k_ref, vT_ref, g_ref, o_ref, m_ref, l_ref, acc_ref, s_ref,
                 *, tk, unroll):
    seq = k_ref.shape[1]
    tq = qT_ref.shape[1]
    l_ref[...] = jnp.zeros(l_ref.shape, F32)
    acc_ref[...] = jnp.zeros(acc_ref.shape, F32)
    bound = bound_ref[0]

    def tiles(j, g):
        off = pl.multiple_of(j * tk, tk)
        kg = k_ref[g, pl.ds(off, tk), :]
        vg = vT_ref[g * HEAD_DIM:(g + 1) * HEAD_DIM, pl.ds(off, tk)]
        return kg, vg

    def scores(kg, h, g):
        r0 = h * HEAD_DIM - g * HEAD_DIM
        return _dot(kg, qT_ref[r0:r0 + KV_WIDTH, :])

    def stage_scores(j, h):
        g = h // Q_PER_KV
        return scores(tiles(j, g)[0], h, g)

    stages = [(jj, h) for jj in range(unroll) for h in range(N_Q_HEADS)]
    depth = s_ref.shape[0]

    def body_bounded(j, carry):
        pending = {}
        for t, (jj, h) in enumerate(stages):
            g = h // Q_PER_KV
            ta = t + depth
            if ta < len(stages):
                jn, hn = stages[ta]
                pending[ta] = stage_scores(j * unroll + jn, hn)
            else:
                jn, hn = stages[ta - len(stages)]
                j_next = jnp.minimum((j + 1) * unroll + jn, seq // tk - 1)
                s_ref[ta - len(stages)] = stage_scores(j_next, hn)
            s = s_ref[t] if t < depth else pending.pop(t)
            p = jnp.exp2(s - bound)
            l_ref[h] += jnp.sum(p.reshape(tk // 8, 8, tq), axis=0)
            rows = slice(h * HEAD_DIM, (h + 1) * HEAD_DIM)
            acc_ref[rows, :] += _dot(tiles(j * unroll + jj, g)[1], p.astype(BF16))
        return carry

    def body_online(j, carry):
        for g in range(N_KV_HEADS):
            kg, vg = tiles(j, g)
            for hh in range(Q_PER_KV):
                h = g * Q_PER_KV + hh
                s = scores(kg, h, g)
                m_prev = m_ref[h:h + 1, :]
                m_new = jnp.maximum(m_prev, jnp.max(s, axis=0, keepdims=True))
                alpha = jnp.exp2(m_prev - m_new)
                p = jnp.exp2(s - m_new)
                l_ref[h] = alpha * l_ref[h] + jnp.sum(p.reshape(tk // 8, 8, tq), axis=0)
                m_ref[h:h + 1, :] = m_new
                rows = slice(h * HEAD_DIM, (h + 1) * HEAD_DIM)
                acc_ref[rows, :] = alpha * acc_ref[rows, :] + _dot(vg, p.astype(BF16))
        return carry

    bounded = bound <= MAX_STATIC_SHIFT

    @pl.when(bounded)
    def _():
        for d in range(depth):
            s_ref[d] = stage_scores(stages[d][0], stages[d][1])
        lax.fori_loop(0, seq // (tk * unroll), body_bounded, 0)

    @pl.when(jnp.logical_not(bounded))
    def _():
        m_ref[...] = jnp.full(m_ref.shape, -1e30, F32)
        lax.fori_loop(0, seq // tk, body_online, 0)

    for h in range(N_Q_HEADS):
        rows = slice(h * HEAD_DIM, (h + 1) * HEAD_DIM)
        inv = 1.0 / jnp.sum(l_ref[h], axis=0, keepdims=True)
        acc_ref[rows, :] = acc_ref[rows, :] * inv
    o = acc_ref[...].T
    o = o * lax.rsqrt(jnp.mean(o * o, axis=-1, keepdims=True) + EPS) * g_ref[...]
    o_ref[...] = o.astype(BF16)


def _attention(bound, qT, k2, vT, g_attn, *, tq, tk, unroll, depth):
    batch, _, seq = qT.shape
    return pl.pallas_call(
        functools.partial(_attn_kernel, tk=tk, unroll=unroll),
        out_shape=jax.ShapeDtypeStruct((batch, seq, ATTN_WIDTH), BF16),
        grid=(batch, seq // tq),
        in_specs=[pl.BlockSpec(memory_space=pltpu.SMEM),
                  pl.BlockSpec((None, ATTN_WIDTH, tq), lambda b, i: (b, 0, i)),
                  pl.BlockSpec((None, N_KV_HEADS, seq, KV_WIDTH), lambda b, i: (b, 0, 0, 0)),
                  pl.BlockSpec((None, KV_WIDTH, seq), lambda b, i: (b, 0, 0)),
                  pl.BlockSpec((1, ATTN_WIDTH), lambda b, i: (0, 0))],
        out_specs=pl.BlockSpec((None, tq, ATTN_WIDTH), lambda b, i: (b, i, 0)),
        scratch_shapes=[pltpu.VMEM((N_Q_HEADS, tq), F32),
                        pltpu.VMEM((N_Q_HEADS, 8, tq), F32),
                        pltpu.VMEM((ATTN_WIDTH, tq), F32),
                        pltpu.VMEM((depth, tk, tq), F32)],
        compiler_params=pltpu.CompilerParams(
            dimension_semantics=("arbitrary", "arbitrary"), vmem_limit_bytes=VMEM_LIMIT_BYTES),
        name="attention",
    )(bound, qT, k2, vT, g_attn)


def _outproj_kernel(a_ref, t_ref, x_ref, wa_ref, wt_ref, gt_ref, g2_ref, sc_ref, sh_ref,
                    x1_ref, h2_ref):
    y = _dot(a_ref[...], wa_ref[...]) + _dot(t_ref[...], wt_ref[...])
    x1 = x_ref[...] + gt_ref[...] * y
    x1_ref[...] = x1
    h = x1 * lax.rsqrt(jnp.mean(x1 * x1, axis=-1, keepdims=True) + EPS) * g2_ref[...]
    h2_ref[...] = (h * (1.0 + sc_ref[...]) + sh_ref[...]).astype(BF16)


def _outproj(attn, tok, x2, w_oa, w_ot, gt1, g2, sc2, sh2, *, seq, tm):
    n, d = x2.shape
    tpb = seq // tm
    const = lambda i: (0, 0)
    per_batch = lambda i: (i // tpb, 0, 0)
    row = lambda i: (i, 0)
    return pl.pallas_call(
        _outproj_kernel,
        out_shape=(jax.ShapeDtypeStruct((n, d), F32), jax.ShapeDtypeStruct((n, d), BF16)),
        grid=(n // tm,),
        in_specs=[pl.BlockSpec((tm, ATTN_WIDTH), row),
                  pl.BlockSpec((tm, TOK_WIDTH), row),
                  pl.BlockSpec((tm, d), row),
                  pl.BlockSpec(w_oa.shape, const, pipeline_mode=pl.Buffered(1)),
                  pl.BlockSpec(w_ot.shape, const, pipeline_mode=pl.Buffered(1)),
                  pl.BlockSpec((None, 1, d), per_batch),
                  pl.BlockSpec((1, d), const),
                  pl.BlockSpec((None, 1, d), per_batch),
                  pl.BlockSpec((None, 1, d), per_batch)],
        out_specs=(pl.BlockSpec((tm, d), row), pl.BlockSpec((tm, d), row)),
        compiler_params=pltpu.CompilerParams(
            dimension_semantics=("arbitrary",), vmem_limit_bytes=VMEM_LIMIT_BYTES),
        name="outproj",
    )(attn, tok, x2, w_oa, w_ot, gt1, g2, sc2, sh2)


def _ffn_kernel(hp_ref, hm_ref, hn_ref, x1_ref, wa_ref, wb_ref, wca_ref, wcb_ref,
                bca_ref, bcb_ref, wd_ref, gt_ref, gf_ref, o_ref, hbuf_ref, acc_ref, z_ref,
                *, tpb, tn):
    tm = hm_ref.shape[0]
    halo = hp_ref.shape[0]
    d_ff = wa_ref.shape[1]
    i = pl.program_id(0)
    first = (i % tpb) == 0
    last = (i % tpb) == tpb - 1
    hbuf_ref[0:halo, :] = jnp.where(first, jnp.zeros_like(hp_ref[...]), hp_ref[...])
    hbuf_ref[halo:halo + tm, :] = hm_ref[...]
    hbuf_ref[halo + tm:, :] = jnp.where(last, jnp.zeros_like(hn_ref[...]), hn_ref[...])
    acc_ref[...] = jnp.zeros(acc_ref.shape, F32)
    rows = tm + 2 * halo

    def conv(z, wc_ref, bc_ref, cols):
        z_prev = pltpu.roll(z, 1, 0)
        z_next = pltpu.roll(z, rows - 1, 0)
        c = (z_prev * wc_ref[0:1, cols] + z * wc_ref[1:2, cols]
             + z_next * wc_ref[2:3, cols] + bc_ref[:, cols])
        return c[halo:halo + tm, :]

    def up(j, slot):
        cols = pl.ds(pl.multiple_of(j * tn, tn), tn)
        hb = hbuf_ref[...]
        z_ref[slot, 0] = _dot(hb, wa_ref[:, cols])
        z_ref[slot, 1] = _dot(hb, wb_ref[:, cols])

    def down(j, slot):
        cols = pl.ds(pl.multiple_of(j * tn, tn), tn)
        a = conv(z_ref[slot, 0], wca_ref, bca_ref, cols)
        b = conv(z_ref[slot, 1], wcb_ref, bcb_ref, cols)
        gated = (a / (1.0 + jnp.exp(-a)) * b).astype(BF16)
        acc_ref[...] += _dot(gated, wd_ref[cols, :])

    nt = d_ff // tn
    up(0, 0)

    def body(i2, carry):
        j = 2 * i2
        up(j + 1, 1)
        down(j, 0)
        up(j + 2, 0)
        down(j + 1, 1)
        return carry

    pairs = (nt - 1) // 2
    lax.fori_loop(0, pairs, body, 0)
    for t in range(2 * pairs, nt):
        slot = (t - 2 * pairs) % 2
        if t + 1 < nt:
            up(t + 1, 1 - slot)
        down(t, slot)

    x2 = x1_ref[...] + gt_ref[...] * acc_ref[...]
    o_ref[...] = x2 * lax.rsqrt(jnp.mean(x2 * x2, axis=-1, keepdims=True) + EPS) * gf_ref[...]


def _ffn(h2, x1, w_ua, w_ub, wc_a, wc_b, bc_a, bc_b, w_down, gt2, g_final, *, seq, tm, tn):
    n, d = x1.shape
    d_ff = w_down.shape[0]
    halo = BF16_SUBLANES
    tpb = seq // tm
    hb = tm // halo
    last_blk = n // halo - 1
    const = lambda i: (0, 0)
    row = lambda i: (i, 0)
    return pl.pallas_call(
        functools.partial(_ffn_kernel, tpb=tpb, tn=tn),
        out_shape=jax.ShapeDtypeStruct((n, d), F32),
        grid=(n // tm,),
        in_specs=[pl.BlockSpec((halo, d), lambda i: (jnp.maximum(i * hb - 1, 0), 0)),
                  pl.BlockSpec((tm, d), row),
                  pl.BlockSpec((halo, d), lambda i: (jnp.minimum((i + 1) * hb, last_blk), 0)),
                  pl.BlockSpec((tm, d), row),
                  pl.BlockSpec((d, d_ff), const, pipeline_mode=pl.Buffered(1)),
                  pl.BlockSpec((d, d_ff), const, pipeline_mode=pl.Buffered(1)),
                  pl.BlockSpec((3, d_ff), const),
                  pl.BlockSpec((3, d_ff), const),
                  pl.BlockSpec((1, d_ff), const),
                  pl.BlockSpec((1, d_ff), const),
                  pl.BlockSpec((d_ff, d), const, pipeline_mode=pl.Buffered(1)),
                  pl.BlockSpec((None, 1, d), lambda i: (i // tpb, 0, 0)),
                  pl.BlockSpec((1, d), const)],
        out_specs=pl.BlockSpec((tm, d), row),
        scratch_shapes=[pltpu.VMEM((tm + 2 * halo, d), BF16),
                        pltpu.VMEM((tm, d), F32),
                        pltpu.VMEM((2, 2, tm + 2 * halo, tn), F32)],
        compiler_params=pltpu.CompilerParams(
            dimension_semantics=("arbitrary",), vmem_limit_bytes=VMEM_LIMIT_BYTES),
        name="ffn",
    )(h2, h2, h2, x1, w_ua, w_ub, wc_a, wc_b, bc_a, bc_b, w_down, gt2, g_final)


def _rope_tables(seq):
    rows = seq // GRID_W
    row_id = jnp.broadcast_to(jnp.arange(rows)[:, None], (rows, GRID_W)).reshape(-1).astype(F32)
    col_id = jnp.broadcast_to(jnp.arange(GRID_W)[None, :], (rows, GRID_W)).reshape(-1).astype(F32)
    axis_dim = HEAD_DIM // 2
    inv_freq = jnp.power(jnp.float32(ROPE_THETA),
                         -jnp.arange(0, axis_dim, 2, dtype=F32) / axis_dim)
    ang_r = row_id[:, None] * inv_freq[None, :]
    ang_c = col_id[:, None] * inv_freq[None, :]
    ang = jnp.concatenate([ang_r, ang_r, ang_c, ang_c], axis=-1)
    cos, sin = jnp.cos(ang), jnp.sin(ang)
    first = (np.arange(HEAD_DIM) % (HEAD_DIM // 2)) < HEAD_DIM // 4
    sin_a = jnp.where(first[None, :], -sin, 0.0)
    sin_b = jnp.where(first[None, :], 0.0, sin)
    rep = LANES // HEAD_DIM
    return jnp.tile(cos, (1, rep)), jnp.tile(sin_a, (1, rep)), jnp.tile(sin_b, (1, rep))


def _block_mean_matrix(width):
    idx = np.arange(width) // HEAD_DIM
    return jnp.asarray((idx[:, None] == idx[None, :]).astype(np.float32) / HEAD_DIM, dtype=BF16)


def kernel(x, c, w_ada, b_ada, g_norm1, w_in, g_q, g_k, g_tok_v, w_s, b_s, g_attn_out,
           g_tok_out, w_out, g_norm2, w_up, w_conv, b_conv, w_down, g_final):
    batch, seq, d = x.shape
    depth = w_ada.shape[0]
    assert depth == 1, "the final RMSNorm is fused into the (single) layer's FFN kernel"
    d_ff = w_down.shape[1]
    n = batch * seq
    tm = 512
    cos, sin_a, sin_b = _rope_tables(seq)
    pm = _block_mean_matrix(256)

    c8 = jnp.zeros((8, d), F32).at[:batch].set(c)
    xf = x.reshape(n, d)
    for l in range(depth):
        mod = _adaln(c8, w_ada[l], b_ada[l][None, :])[:batch]
        sh1, sc1, gt1, sh2, sc2, gt2 = [m[:, None, :] for m in jnp.split(mod, 6, axis=-1)]

        tile_h = lambda g, w: jnp.tile(g, w // HEAD_DIM)[None, :]
        ws2 = jnp.concatenate([w_s[l][0::2], w_s[l][1::2]], axis=-1).astype(BF16)
        bs2 = jnp.repeat(b_s[l].reshape(N_TOK_HEADS // 2, 2, CHUNK), HEAD_DIM, axis=1)
        bs2 = jnp.swapaxes(bs2, 1, 2)

        qT, k2, vT, tok = _inproj(
            xf, sh1, sc1, g_norm1[l][None, :], w_in[l].astype(BF16), cos, sin_a, sin_b,
            tile_h(g_q[l], ATTN_WIDTH), tile_h(g_k[l], KV_WIDTH), tile_h(g_tok_v[l], TOK_WIDTH),
            pm, ws2, bs2, g_tok_out[l][None, :], batch=batch, seq=seq, tm=tm)

        bound = (Q_SCALE * HEAD_DIM * BF16_ROUNDING_MARGIN
                 * jnp.max(jnp.abs(g_q[l])) * jnp.max(jnp.abs(g_k[l]))).reshape(1).astype(F32)
        attn = _attention(bound, qT, k2, vT, g_attn_out[l][None, :], tq=512, tk=256, unroll=4, depth=2)

        w_o = w_out[l].astype(BF16)
        x1, h2 = _outproj(attn.reshape(n, ATTN_WIDTH), tok, xf, w_o[:ATTN_WIDTH], w_o[ATTN_WIDTH:],
                          gt1, g_norm2[l][None, :], sc2, sh2, seq=seq, tm=tm)

        w_u = w_up[l].astype(BF16)
        xf = _ffn(h2, x1, w_u[:, :d_ff], w_u[:, d_ff:], w_conv[l][:, :d_ff], w_conv[l][:, d_ff:],
                  b_conv[l][None, :d_ff], b_conv[l][None, d_ff:], w_down[l].astype(BF16),
                  gt2, g_final[None, :], seq=seq, tm=tm, tn=256)
    return xf.reshape(batch, seq, d)
```
